```python
import jax, jax.numpy as jnp
from jax import lax
import numpy as np

D_MODEL = 1024
BATCH = 8
SEQ = 8192
DEPTH = 1
DEC_BATCH = 128
DEC_SEQ = 1
PAST_LEN = 8192
PAGE_SIZE = 128

HEAD_DIM = 64
NSA_HEADS = 8
NSA_KV = 2
NSA_GROUP = NSA_HEADS // NSA_KV
SB_HEADS = 8
L_CMP = 32
D_CMP = 16
L_SEL = 64
N_SEL = 16
N_LOCAL = 2
WINDOW = 512
D_FF = 4 * D_MODEL
D_PLE = 256
ROPE_THETA = 10000.0
Q_BLOCK = 128
EPS = 1e-6
FORCE_BONUS = 1e4
NEG_INF = -1e30
NSA_QW = NSA_HEADS * HEAD_DIM
NSA_KVW = NSA_KV * HEAD_DIM
SB_W = SB_HEADS * HEAD_DIM
SPLITS = (NSA_QW, 6 * NSA_KVW, 3 * NSA_HEADS, 3 * SB_W, 2 * D_MODEL)
IN_COLS = sum(SPLITS)

kernel_name = 'nsa_stickbreaking_parallel_decoder_step'


def rms_norm(x, g):
    xf = x.astype(jnp.float32)
    y = xf * lax.rsqrt(jnp.mean(xf * xf, axis=-1, keepdims=True) + EPS)
    return (y * g.astype(jnp.float32)).astype(x.dtype)


def rope(x, pos):
    half = HEAD_DIM // 2
    inv = jnp.power(ROPE_THETA, -jnp.arange(half, dtype=jnp.float32) * (2.0 / HEAD_DIM))
    ang = pos.astype(jnp.float32)[:, None] * inv[None, :]
    shape = (ang.shape[0],) + (1,) * (x.ndim - 3) + (half,)
    cos = jnp.cos(ang).reshape(shape)
    sin = jnp.sin(ang).reshape(shape)
    xf = x.astype(jnp.float32)
    x1, x2 = xf[..., :half], xf[..., half:]
    return jnp.concatenate([x1 * cos - x2 * sin, x2 * cos + x1 * sin], axis=-1).astype(x.dtype)


def project(h, pos, w_in):
    B, T = h.shape[:2]
    z = h @ w_in
    o = [int(v) for v in np.cumsum((0,) + SPLITS)]
    q_a = rope(z[..., o[0]:o[1]].reshape(B, T, NSA_HEADS, HEAD_DIM), pos)
    kv = z[..., o[1]:o[2]].reshape(B, T, 3, 2, NSA_KV, HEAD_DIM)
    keys = rope(kv[:, :, :, 0], pos)
    kv_a = jnp.stack([keys, kv[:, :, :, 1]], axis=3).reshape(B, T, 6, NSA_KV, HEAD_DIM)
    gate_a = z[..., o[2]:o[3]].reshape(B, T, NSA_HEADS, 3)
    sb = z[..., o[3]:o[4]].reshape(B, T, 3, SB_HEADS, HEAD_DIM)
    mg = z[..., o[4]:o[5]].reshape(B, T, 2, D_MODEL)
    return q_a, kv_a, gate_a, sb, mg


def compress(rows, w_cmp, pe_cmp):
    n_cmp = (rows.shape[1] - L_CMP) // D_CMP + 1
    idx = np.arange(n_cmp)[:, None] * D_CMP + np.arange(L_CMP)[None, :]
    blocks = rows[:, idx] + pe_cmp[:, :, None, :]
    return jnp.einsum('bnlcgd,clde->bncge', blocks, w_cmp)


def cmp_to_sel(n_cmp, n_sel):
    s = np.arange(n_cmp)[:, None] * D_CMP
    b = np.arange(n_sel)[None, :] * L_SEL
    return ((s < b + L_SEL) & (s + L_CMP > b)).astype(np.float32)


def gather_blocks(blocks, ids):
    return jax.vmap(jax.vmap(lambda b, i: b[i]))(blocks, ids)


def nsa_attend(q, pos_q, kc, vc, ks_blk, vs_blk, kw, vw, pos_w, gate):
    B, Tq = q.shape[:2]
    scale = HEAD_DIM ** -0.5
    qg = q.reshape(B, Tq, NSA_KV, NSA_GROUP, HEAD_DIM)
    t = pos_q[:, None]
    n_cmp = kc.shape[1]
    m_cmp = (jnp.arange(n_cmp) * D_CMP + (L_CMP - 1))[None, :] <= t
    s = jnp.einsum('btgjd,bngd->bgjtn', qg, kc, preferred_element_type=jnp.float32) * scale
    p_cmp = jax.nn.softmax(jnp.where(m_cmp, s, NEG_INF), axis=-1) * m_cmp
    o_cmp = jnp.einsum('bgjtn,bngd->btgjd', p_cmp.astype(vc.dtype), vc)
    n_sel = ks_blk.shape[1]
    imp = jnp.einsum('bgjtn,ns->bgts', p_cmp, jnp.asarray(cmp_to_sel(n_cmp, n_sel)))
    blk = jnp.arange(n_sel)[None, :]
    cur = t // L_SEL
    forced = (blk == 0) | ((blk <= cur) & (blk > cur - N_LOCAL))
    valid = blk * L_SEL <= t
    imp = jnp.where(valid, jnp.where(forced, imp + FORCE_BONUS, imp), NEG_INF)
    _, sel = lax.top_k(imp, min(N_SEL, n_sel))
    kg = gather_blocks(jnp.moveaxis(ks_blk, 3, 1), sel)
    vg = gather_blocks(jnp.moveaxis(vs_blk, 3, 1), sel)
    pos_sel = sel[..., None] * L_SEL + jnp.arange(L_SEL)
    m_sel = (pos_sel <= pos_q[None, None, :, None, None])[:, :, None]
    s = jnp.einsum('btgjd,bgtnld->bgjtnl', qg, kg, preferred_element_type=jnp.float32) * scale
    p_sel = jax.nn.softmax(jnp.where(m_sel, s, NEG_INF), axis=(-2, -1))
    o_sel = jnp.einsum('bgjtnl,bgtnld->btgjd', p_sel.astype(vg.dtype), vg)
    pw = pos_w[None, :]
    m_w = (pw <= t) & (pw > t - WINDOW) & (pw >= 0)
    s = jnp.einsum('btgjd,blgd->bgjtl', qg, kw, preferred_element_type=jnp.float32) * scale
    p_w = jax.nn.softmax(jnp.where(m_w, s, NEG_INF), axis=-1)
    o_w = jnp.einsum('bgjtl,blgd->btgjd', p_w.astype(vw.dtype), vw)
    g = jax.nn.sigmoid(gate).reshape(B, Tq, NSA_KV, NSA_GROUP, 3, 1)
    o = g[:, :, :, :, 0] * o_cmp + g[:, :, :, :, 1] * o_sel + g[:, :, :, :, 2] * o_w
    return o.reshape(B, Tq, NSA_QW)


def sb_attend(q, pos_q, k, v, pos_k):
    z = jnp.einsum('bthd,blhd->bhtl', q, k, preferred_element_type=jnp.float32) * (HEAD_DIM ** -0.5)
    m = pos_k[None, :] < pos_q[:, None]
    log_1m = jnp.where(m, jax.nn.log_sigmoid(-z), 0.0)
    after = lax.cumsum(log_1m, axis=3, reverse=True) - log_1m
    a = jnp.where(m, jnp.exp(jax.nn.log_sigmoid(z) + after), 0.0)
    return jnp.einsum('bhtl,blhd->bthd', a.astype(v.dtype), v)


def block_tail(x, o_a, o_b, mg, p, w_oa, w_ob, w_out, g_post_mix, g_pre_ffn, w_up, w_down, g_post_ffn, w_ple, w_ple_gate):
    ya = o_a @ w_oa
    yb = o_b.reshape(o_b.shape[:2] + (SB_W,)) @ w_ob
    merged = jax.nn.sigmoid(mg[:, :, 0]) * ya + jax.nn.sigmoid(mg[:, :, 1]) * yb
    x = x + rms_norm(merged @ w_out, g_post_mix)
    h = rms_norm(x, g_pre_ffn)
    f = jnp.square(jax.nn.relu(h @ w_up)) @ w_down
    x = x + rms_norm(f, g_post_ffn)
    return x + (p @ w_ple) * jax.nn.sigmoid(x @ w_ple_gate)


def setup_inputs(seed: int = 0) -> dict:
    key = jax.random.key(seed)
    ks = jax.random.split(key, 24)
    f32 = jnp.float32
    n_pages = PAST_LEN // PAGE_SIZE
    used = DEC_BATCH * n_pages
    n_phys = used + max(1, used // 4)
    w_buf = min(WINDOW, PAST_LEN)

    def nrm(k, shape, scale):
        return jax.random.normal(k, shape, f32) * scale

    def gain(k):
        return 1.0 + nrm(k, (DEPTH, D_MODEL), 0.05)

    page_table = jax.random.permutation(ks[7], n_phys)[:used].reshape(DEC_BATCH, n_pages).astype(jnp.int32)
    return {
        'x_prompt': nrm(ks[0], (BATCH, SEQ, D_MODEL), 1.0),
        'x_sample': nrm(ks[1], (DEC_BATCH, DEC_SEQ, D_MODEL), 1.0),
        'p_prompt': nrm(ks[2], (DEPTH, BATCH, SEQ, D_PLE), 1.0),
        'p_sample': nrm(ks[3], (DEPTH, DEC_BATCH, DEC_SEQ, D_PLE), 1.0),
        'cache_nsa': nrm(ks[4], (DEPTH, n_phys, PAGE_SIZE, 4, NSA_KV, HEAD_DIM), 1.0),
        'cache_sb': nrm(ks[5], (DEPTH, n_phys, PAGE_SIZE, 2, SB_HEADS, HEAD_DIM), 1.0),
        'state_win': nrm(ks[6], (DEPTH, DEC_BATCH, w_buf, 2, NSA_KV, HEAD_DIM), 1.0),
        'page_table': page_table,
        'w_in': nrm(ks[8], (DEPTH, D_MODEL, IN_COLS), D_MODEL ** -0.5),
        'w_cmp': nrm(ks[9], (DEPTH, 2, L_CMP, HEAD_DIM, HEAD_DIM), (L_CMP * HEAD_DIM) ** -0.5),
        'pe_cmp': nrm(ks[10], (DEPTH, L_CMP, 2, HEAD_DIM), 0.1),
        'w_oa': nrm(ks[11], (DEPTH, NSA_QW, D_MODEL), NSA_QW ** -0.5),
        'w_ob': nrm(ks[12], (DEPTH, SB_W, D_MODEL), SB_W ** -0.5),
        'w_out': nrm(ks[13], (DEPTH, D_MODEL, D_MODEL), D_MODEL ** -0.5),
        'g_pre_mix': gain(ks[14]),
        'g_post_mix': gain(ks[15]),
        'g_pre_ffn': gain(ks[16]),
        'g_post_ffn': gain(ks[17]),
        'w_up': nrm(ks[18], (DEPTH, D_MODEL, D_FF), D_MODEL ** -0.5),
        'w_down': nrm(ks[19], (DEPTH, D_FF, D_MODEL), D_FF ** -0.5),
        'w_ple': nrm(ks[20], (DEPTH, D_PLE, D_MODEL), D_PLE ** -0.5),
        'w_ple_gate': nrm(ks[21], (DEPTH, D_MODEL, D_MODEL), D_MODEL ** -0.5),
    }


def reference(x_prompt, x_sample, p_prompt, p_sample, cache_nsa, cache_sb, state_win, page_table, w_in, w_cmp, pe_cmp, w_oa, w_ob, w_out, g_pre_mix, g_post_mix, g_pre_ffn, g_post_ffn, w_up, w_down, w_ple, w_ple_gate):
    B, T = x_prompt.shape[:2]
    DB, TS = x_sample.shape[:2]
    n_pages = page_table.shape[1]
    past = n_pages * cache_nsa.shape[2]
    w_buf = state_win.shape[2]
    w_prompt = min(WINDOW, T)
    pos_p = jnp.arange(T, dtype=jnp.int32)
    pos_s = past + jnp.arange(TS, dtype=jnp.int32)
    L = past + TS
    pos_k = jnp.arange(L, dtype=jnp.int32)
    pos_w_s = past - w_buf + jnp.arange(w_buf + TS, dtype=jnp.int32)
    n_sel_s = -(-L // L_SEL)
    starts = jnp.arange(T // Q_BLOCK, dtype=jnp.int32) * Q_BLOCK
    x_p, x_s = x_prompt, x_sample
    nsa_p, sb_p, win_p, nsa_s, sb_s_new, win_s = [], [], [], [], [], []
    for i in range(DEPTH):
        tail_w = (w_oa[i], w_ob[i], w_out[i], g_post_mix[i], g_pre_ffn[i], w_up[i], w_down[i], g_post_ffn[i], w_ple[i], w_ple_gate[i])
        h = rms_norm(x_p, g_pre_mix[i])
        q_a, kv_a, gate_a, sb, mg = project(h, pos_p, w_in[i])
        kc = compress(kv_a[:, :, 0:2], w_cmp[i], pe_cmp[i])
        kc_k, kc_v = kc[:, :, 0], kc[:, :, 1]
        ks_blk = kv_a[:, :, 2].reshape(B, T // L_SEL, L_SEL, NSA_KV, HEAD_DIM)
        vs_blk = kv_a[:, :, 3].reshape(B, T // L_SEL, L_SEL, NSA_KV, HEAD_DIM)
        win_pad = jnp.pad(kv_a[:, :, 4:6], ((0, 0), (WINDOW, 0), (0, 0), (0, 0), (0, 0)))
        sb_q, sb_k, sb_v = sb[:, :, 0], sb[:, :, 1], sb[:, :, 2]

        def nsa_block(start):
            qb = lax.dynamic_slice_in_dim(q_a, start, Q_BLOCK, axis=1)
            gb = lax.dynamic_slice_in_dim(gate_a, start, Q_BLOCK, axis=1)
            wb = lax.dynamic_slice_in_dim(win_pad, start, WINDOW + Q_BLOCK, axis=1)
            pos_q = start + jnp.arange(Q_BLOCK, dtype=jnp.int32)
            pos_w = start - WINDOW + jnp.arange(WINDOW + Q_BLOCK, dtype=jnp.int32)
            return nsa_attend(qb, pos_q, kc_k, kc_v, ks_blk, vs_blk, wb[:, :, 0], wb[:, :, 1], pos_w, gb)

        def sb_block(start):
            qb = lax.dynamic_slice_in_dim(sb_q, start, Q_BLOCK, axis=1)
            pos_q = start + jnp.arange(Q_BLOCK, dtype=jnp.int32)
            return sb_attend(qb, pos_q, sb_k, sb_v, pos_p)

        o_a = jnp.moveaxis(lax.map(nsa_block, starts), 0, 1).reshape(B, T, NSA_QW)
        o_b = jnp.moveaxis(lax.map(sb_block, starts), 0, 1).reshape(B, T, SB_HEADS, HEAD_DIM)
        nsa_p.append(kv_a[:, :, 0:4])
        sb_p.append(sb[:, :, 1:3])
        win_p.append(kv_a[:, T - w_prompt:, 4:6])
        x_p = block_tail(x_p, o_a, o_b, mg, p_prompt[i], *tail_w)
        h = rms_norm(x_s, g_pre_mix[i])
        q_s, kv_s, gate_s, sb_s, mg_s = project(h, pos_s, w_in[i])
        win_cat = jnp.concatenate([state_win[i], kv_s[:, :, 4:6]], axis=1)

        def per_seq(args):
            pt, q1, g1, nsa_new, sb1, win1 = args
            rows = jnp.concatenate([cache_nsa[i, pt].reshape(past, 4, NSA_KV, HEAD_DIM), nsa_new], axis=0)[None]
            kc1 = compress(rows[:, :, 0:2], w_cmp[i], pe_cmp[i])
            sel = jnp.pad(rows[:, :, 2:4], ((0, 0), (0, n_sel_s * L_SEL - L), (0, 0), (0, 0), (0, 0)))
            sel = sel.reshape(1, n_sel_s, L_SEL, 2, NSA_KV, HEAD_DIM)
            o1 = nsa_attend(q1[None], pos_s, kc1[:, :, 0], kc1[:, :, 1], sel[:, :, :, 0], sel[:, :, :, 1], win1[None, :, 0], win1[None, :, 1], pos_w_s, g1[None])
            kv_sb = jnp.concatenate([cache_sb[i, pt].reshape(past, 2, SB_HEADS, HEAD_DIM), sb1[:, 1:3]], axis=0)[None]
            o2 = sb_attend(sb1[None, :, 0], pos_s, kv_sb[:, :, 0], kv_sb[:, :, 1], pos_k)
            return o1[0], o2[0]

        o_a_s, o_b_s = lax.map(per_seq, (page_table, q_s, gate_s, kv_s[:, :, 0:4], sb_s, win_cat))
        nsa_s.append(kv_s[:, :, 0:4])
        sb_s_new.append(sb_s[:, :, 1:3])
        win_s.append(win_cat[:, TS:])
        x_s = block_tail(x_s, o_a_s, o_b_s, mg_s, p_sample[i], *tail_w)
    return (x_p, x_s, jnp.stack(nsa_p), jnp.stack(sb_p), jnp.stack(win_p), jnp.stack(nsa_s), jnp.stack(sb_s_new), jnp.stack(win_s))
```

```python
import functools

import numpy as np
import jax
import jax.numpy as jnp
from jax import lax
from jax.experimental import pallas as pl
from jax.experimental.pallas import tpu as pltpu

F32 = jnp.float32
BF16 = jnp.bfloat16

HEAD_DIM = 64
NSA_HEADS = 8
NSA_KV = 2
NSA_GROUP = NSA_HEADS // NSA_KV
SB_HEADS = 8
L_CMP = 32
D_CMP = 16
L_SEL = 64
N_SEL = 16
N_LOCAL = 2
WINDOW = 512
ROPE_THETA = 10000.0
EPS = 1e-6
FORCE_BONUS = 1e4
NEG_INF = -1e30
LOWEST = -3e38
SCALE = HEAD_DIM ** -0.5
NSA_QW = NSA_HEADS * HEAD_DIM
NSA_KVW = NSA_KV * HEAD_DIM
SB_W = SB_HEADS * HEAD_DIM
GATE_W = 3 * NSA_HEADS

LANES = 128
TOK_PER_ROW = 16
VMEM_LIMIT = 56 * 1024 * 1024


def _cparams(sem):
    return pltpu.CompilerParams(dimension_semantics=sem, vmem_limit_bytes=VMEM_LIMIT)


def _dot(a, b):
    return jnp.dot(a, b, preferred_element_type=F32)


def _nt(a, b):
    return lax.dot_general(a, b, (((1,), (1,)), ((), ())), preferred_element_type=F32)


def _row_dot(a, row):
    return jnp.sum(a.astype(F32) * row.astype(BF16).astype(F32), axis=1, keepdims=True)


def _rms(x, g):
    return x * lax.rsqrt(jnp.mean(x * x, axis=-1, keepdims=True) + EPS) * g


def _sigmoid(x):
    return 1.0 / (1.0 + jnp.exp(-x))


def _masked_softmax(s, mask):
    s = jnp.where(mask, s, NEG_INF)
    e = jnp.exp(s - jnp.max(s, axis=-1, keepdims=True))
    return e / jnp.sum(e, axis=-1, keepdims=True)


def _proj_kernel(x_ref, g_ref, w_ref, cos_ref, sin_ref,
                 q_ref, nsa_ref, win_ref, gate_ref, sbq_ref, sbkv_ref,
                 ksel_ref, vsel_ref, kwin_ref, vwin_ref, sbk_ref, sbv_ref):
    h = _rms(x_ref[...], g_ref[...]).astype(BF16)
    cos = cos_ref[...]
    sin = sin_ref[...]
    lane = lax.broadcasted_iota(jnp.int32, cos.shape, 1)
    first_half = (lane & (HEAD_DIM - 1)) < HEAD_DIM // 2

    def rope(z):
        other = jnp.where(first_half, pltpu.roll(z, LANES - HEAD_DIM // 2, 1), pltpu.roll(z, HEAD_DIM // 2, 1))
        return z * cos + other * sin

    def blk(z, c):
        return z[:, c * LANES:(c + 1) * LANES]

    zq = _dot(h, w_ref[:, 0:NSA_QW])
    q = jnp.concatenate([rope(blk(zq, c)) for c in range(NSA_QW // LANES)], axis=1)
    q_ref[...] = (q * SCALE).astype(BF16)

    o = NSA_QW
    zkv = _dot(h, w_ref[:, o:o + 6 * NSA_KVW])
    kc, vc, ks, vs, kw, vw = rope(blk(zkv, 0)), blk(zkv, 1), rope(blk(zkv, 2)), blk(zkv, 3), rope(blk(zkv, 4)), blk(zkv, 5)
    nsa_ref[...] = jnp.concatenate([kc, vc, ks, vs], axis=1)
    win_ref[...] = jnp.concatenate([kw, vw], axis=1)
    ksel_ref[...] = ks.astype(BF16)
    vsel_ref[...] = vs.astype(BF16)
    kwin_ref[...] = kw.astype(BF16)
    vwin_ref[...] = vw.astype(BF16)

    o += 6 * NSA_KVW
    zsb = _dot(h, w_ref[:, o:o + 3 * SB_W])
    sbq_ref[...] = (zsb[:, 0:SB_W] * SCALE).astype(BF16)
    sbkv_ref[...] = zsb[:, SB_W:3 * SB_W]
    sbk_ref[...] = zsb[:, SB_W:2 * SB_W].astype(BF16)
    sbv_ref[...] = zsb[:, 2 * SB_W:3 * SB_W].astype(BF16)

    o += 3 * SB_W
    gate_ref[...] = _dot(h, w_ref[:, o:o + LANES])


def _project(x2d, g, w_proj, cos, sin, tm):
    n, d = x2d.shape
    n_tab = cos.shape[0] // tm
    row = lambda i: (i, 0)
    const = lambda i: (0, 0)
    tab = lambda i: (i % n_tab, 0)
    widths = [(NSA_QW, BF16), (4 * NSA_KVW, F32), (2 * NSA_KVW, F32), (LANES, F32), (SB_W, BF16), (2 * SB_W, F32),
              (NSA_KVW, BF16), (NSA_KVW, BF16), (NSA_KVW, BF16), (NSA_KVW, BF16), (SB_W, BF16), (SB_W, BF16)]
    return pl.pallas_call(
        _proj_kernel,
        grid=(n // tm,),
        in_specs=[pl.BlockSpec((tm, d), row), pl.BlockSpec((1, d), const),
                  pl.BlockSpec(w_proj.shape, const, pipeline_mode=pl.Buffered(1)),
                  pl.BlockSpec((tm, LANES), tab), pl.BlockSpec((tm, LANES), tab)],
        out_specs=[pl.BlockSpec((tm, wd), row) for wd, _ in widths],
        out_shape=[jax.ShapeDtypeStruct((n, wd), dt) for wd, dt in widths],
        compiler_params=_cparams(("parallel",)),
        name="proj",
    )(x2d, g, w_proj, cos, sin)


def _compress_rows(x_of, wlo_ref, whi_ref, pelo_ref, pehi_ref, rows):
    ylo = jnp.zeros((rows, 2 * NSA_KVW), F32)
    yhi = jnp.zeros((rows, 2 * NSA_KVW), F32)
    for l in range(TOK_PER_ROW):
        xl = x_of(l)
        ylo = ylo + _dot((xl + pelo_ref[l]).astype(BF16), wlo_ref[l])
        yhi = yhi + _dot((xl + pehi_ref[l]).astype(BF16), whi_ref[l])
    return ylo, yhi


def _compress_kernel(x_ref, wlo_ref, whi_ref, pelo_ref, pehi_ref, out_ref, ylo_ref, yhi_ref):
    t = pl.program_id(1)
    rows = x_ref.shape[1]
    n2 = ylo_ref.shape[0]
    feat = 4 * NSA_KVW
    ylo, yhi = _compress_rows(lambda l: x_ref[0, :, l * feat:l * feat + 2 * NSA_KVW],
                              wlo_ref, whi_ref, pelo_ref, pehi_ref, rows)
    r0 = pl.multiple_of(t * rows, rows)
    ylo_ref[pl.ds(r0, rows), :] = ylo
    yhi_ref[pl.ds(r0, rows), :] = yhi

    @pl.when(t == pl.num_programs(1) - 1)
    def _():
        out_ref[0] = (ylo_ref[...] + pltpu.roll(yhi_ref[...], n2 - 1, 0)).astype(BF16)


def _compress(nsa_rows, wlo, whi, pelo, pehi, rows):
    b, n2, wd = nsa_rows.shape
    c3 = lambda i, t: (0, 0, 0)
    return pl.pallas_call(
        _compress_kernel,
        grid=(b, n2 // rows),
        in_specs=[pl.BlockSpec((1, rows, wd), lambda i, t: (i, t, 0)),
                  pl.BlockSpec(wlo.shape, c3), pl.BlockSpec(whi.shape, c3),
                  pl.BlockSpec(pelo.shape, c3), pl.BlockSpec(pehi.shape, c3)],
        out_specs=pl.BlockSpec((1, n2, 2 * NSA_KVW), lambda i, t: (i, 0, 0)),
        out_shape=jax.ShapeDtypeStruct((b, n2, 2 * NSA_KVW), BF16),
        scratch_shapes=[pltpu.VMEM((n2, 2 * NSA_KVW), F32), pltpu.VMEM((n2, 2 * NSA_KVW), F32)],
        compiler_params=_cparams(("parallel", "arbitrary")),
        name="compress",
    )(nsa_rows, wlo, whi, pelo, pehi)


def _select_blocks(imp, tq, n_sel):
    w = imp.shape[1]
    blk = lax.broadcasted_iota(jnp.int32, (1, w), 1)
    cur = tq // L_SEL
    forced = (blk == 0) | ((blk <= cur) & (blk > cur - N_LOCAL))
    valid = (blk * L_SEL <= tq) & (blk < n_sel)
    x = jnp.where(valid, jnp.where(forced, imp + FORCE_BONUS, imp), NEG_INF)
    x = jnp.where(blk < n_sel, x, LOWEST)
    sel = jnp.zeros(x.shape, jnp.bool_)
    for _ in range(min(N_SEL, n_sel)):
        m = jnp.max(x, axis=1, keepdims=True)
        first = jnp.min(jnp.where(x == m, blk, w), axis=1, keepdims=True)
        hit = blk == first
        sel = sel | hit
        x = jnp.where(hit, LOWEST, x)
    return sel & valid


NSA_TQ = 128
NSA_KC = 512


def _nsa_kernel(q_ref, gate_ref, kc_ref, ksel_ref, vsel_ref, kwin_ref, vwin_ref, c2s_ref, et_ref, o_ref, *, seq):
    i = pl.program_id(1)
    tq_n = NSA_TQ
    q0 = i * tq_n
    n2 = kc_ref.shape[1]
    n_cmp = n2 - 1
    n_sel = seq // L_SEL
    wsel = c2s_ref.shape[1]
    rows = NSA_GROUP * tq_n

    qf = q_ref[0].astype(F32)
    gate = _sigmoid(gate_ref[0])
    lane = lax.broadcasted_iota(jnp.int32, (tq_n, LANES), 1)
    tq = q0 + lax.broadcasted_iota(jnp.int32, (tq_n, 1), 0)
    t4 = jnp.concatenate([tq] * NSA_GROUP, axis=0)
    blkid = lax.broadcasted_iota(jnp.int32, (1, wsel), 1)

    def stack4(a):
        return jnp.concatenate([a] * NSA_GROUP, axis=0)

    def online(carry, s, v):
        m, l, acc = carry
        mn = jnp.maximum(m, jnp.max(s, axis=1, keepdims=True))
        a = jnp.exp(m - mn)
        p = jnp.exp(s - mn)
        return mn, a * l + jnp.sum(p, axis=1, keepdims=True), a * acc + _dot(p.astype(BF16), v)

    outs = [None] * NSA_HEADS
    for g in range(NSA_KV):
        in_group = (lane // HEAD_DIM) == g
        embs = []
        for j in range(NSA_GROUP):
            hd = NSA_GROUP * g + j
            b = qf[:, (hd // 2) * LANES:(hd // 2 + 1) * LANES]
            if hd % 2 != g:
                b = pltpu.roll(b, HEAD_DIM, 1)
            embs.append(jnp.where(in_group, b, 0.0))
        q4 = jnp.concatenate(embs, axis=0).astype(BF16)

        s = _nt(q4, kc_ref[0, :, 0:LANES])
        ncol = lax.broadcasted_iota(jnp.int32, (1, n2), 1)
        mc = (ncol * D_CMP + (L_CMP - 1) <= t4) & (ncol < n_cmp)
        p = jnp.where(mc, _masked_softmax(s, mc), 0.0)
        o_cmp = _dot(p.astype(BF16), kc_ref[0, :, LANES:2 * LANES])
        psum = p[0:tq_n]
        for j in range(1, NSA_GROUP):
            psum = psum + p[j * tq_n:(j + 1) * tq_n]
        imp = jnp.dot(psum, c2s_ref[...], preferred_element_type=F32, precision=lax.Precision.HIGHEST)

        sel = _select_blocks(imp, tq, n_sel)
        notsel = jnp.where(sel, 0.0, 1.0)
        notsel_main = jnp.where(blkid >= q0 // L_SEL, 1.0, notsel)
        aug_main = jnp.concatenate([q4, stack4(notsel_main.astype(BF16))], axis=1)
        aug_diag = jnp.concatenate([q4, stack4(notsel.astype(BF16))], axis=1)

        def main_body(c, carry):
            k0 = pl.multiple_of(c * NSA_KC, NSA_KC)
            rhs = jnp.concatenate([ksel_ref[0, pl.ds(k0, NSA_KC), :], et_ref[pl.ds(k0, NSA_KC), :]], axis=1)
            return online(carry, _nt(aug_main, rhs), vsel_ref[0, pl.ds(k0, NSA_KC), :])

        init = (jnp.full((rows, 1), LOWEST, F32), jnp.zeros((rows, 1), F32), jnp.zeros((rows, LANES), F32))
        carry = lax.fori_loop(0, (q0 + NSA_KC - 1) // NSA_KC, main_body, init)
        kd = pl.multiple_of(q0, tq_n)
        rhs = jnp.concatenate([ksel_ref[0, pl.ds(kd, tq_n), :], et_ref[pl.ds(kd, tq_n), :]], axis=1)
        kpos = q0 + lax.broadcasted_iota(jnp.int32, (1, tq_n), 1)
        s = jnp.where(kpos <= t4, _nt(aug_diag, rhs), NEG_INF)
        _, l, acc = online(carry, s, vsel_ref[0, pl.ds(kd, tq_n), :])
        o_sel = acc / l

        wlen = WINDOW + tq_n
        w0 = pl.multiple_of(jnp.maximum(q0 - WINDOW, 0), tq_n)
        s = _nt(q4, kwin_ref[0, pl.ds(w0, wlen), :])
        kpos = w0 + lax.broadcasted_iota(jnp.int32, (1, wlen), 1)
        mw = (kpos <= t4) & (kpos > t4 - WINDOW)
        o_w = _dot(_masked_softmax(s, mw).astype(BF16), vwin_ref[0, pl.ds(w0, wlen), :])

        for j in range(NSA_GROUP):
            hd = NSA_GROUP * g + j
            r = slice(j * tq_n, (j + 1) * tq_n)
            o = (gate[:, 3 * hd:3 * hd + 1] * o_cmp[r] + gate[:, 3 * hd + 1:3 * hd + 2] * o_sel[r]
                 + gate[:, 3 * hd + 2:3 * hd + 3] * o_w[r])
            if hd % 2 != g:
                o = pltpu.roll(o, HEAD_DIM, 1)
            outs[hd] = o
    pairs = [jnp.where(lane < HEAD_DIM, outs[2 * p], outs[2 * p + 1]) for p in range(NSA_HEADS // 2)]
    o_ref[0] = jnp.concatenate(pairs, axis=1).astype(BF16)


def _nsa_prompt(q, gate, kc, ksel, vsel, kwin, vwin, c2s, et):
    b, t, _ = q.shape
    assert t % NSA_KC == 0 and t >= WINDOW + NSA_TQ
    blk = lambda i, j: (i, j, 0)
    per_b = lambda i, j: (i, 0, 0)
    c2 = lambda i, j: (0, 0)
    kv_spec = pl.BlockSpec((1, t, LANES), per_b)
    return pl.pallas_call(
        functools.partial(_nsa_kernel, seq=t),
        grid=(b, t // NSA_TQ),
        in_specs=[pl.BlockSpec((1, NSA_TQ, NSA_QW), blk), pl.BlockSpec((1, NSA_TQ, LANES), blk),
                  pl.BlockSpec((1,) + kc.shape[1:], per_b), kv_spec, kv_spec, kv_spec, kv_spec,
                  pl.BlockSpec(c2s.shape, c2), pl.BlockSpec(et.shape, c2)],
        out_specs=pl.BlockSpec((1, NSA_TQ, NSA_QW), blk),
        out_shape=jax.ShapeDtypeStruct((b, t, NSA_QW), BF16),
        compiler_params=_cparams(("parallel", "arbitrary")),
        name="nsa_prompt",
    )(q, gate, kc, ksel, vsel, kwin, vwin, c2s, et)


SB_T = 256


def _softplus(z):
    return jnp.maximum(z, 0.0) + jnp.log(1.0 + jnp.exp(-jnp.abs(z)))


def _sb_kernel(q_ref, k_ref, v_ref, u_ref, o_ref):
    i = pl.program_id(2)
    tb = SB_T
    q = q_ref[0].astype(F32)
    lane = lax.broadcasted_iota(jnp.int32, (tb, LANES), 1)
    u = u_ref[...]
    qpos = lax.broadcasted_iota(jnp.int32, (tb, 1), 0)
    kpos = lax.broadcasted_iota(jnp.int32, (1, tb), 1)
    below = kpos < qpos

    def chunk(c, carry, acc, qh, diagonal):
        k0 = pl.multiple_of(c * tb, tb)
        z = _nt(qh, k_ref[0, pl.ds(k0, tb), :])
        sp = _softplus(z)
        lm = -sp
        if diagonal:
            lm = jnp.where(below, lm, 0.0)
        hi = lm.astype(BF16)
        lo = (lm - hi.astype(F32)).astype(BF16)
        after = _dot(hi, u) + _dot(lo, u) + carry
        a = jnp.exp(z - sp + after)
        if diagonal:
            a = jnp.where(below, a, 0.0)
        acc = acc + _dot(a.astype(BF16), v_ref[0, pl.ds(k0, tb), :])
        return carry + jnp.sum(lm, axis=1, keepdims=True), acc

    outs = []
    for hh in range(2):
        qh = jnp.where((lane // HEAD_DIM) == hh, q, 0.0).astype(BF16)
        carry, acc = chunk(i, jnp.zeros((tb, 1), F32), jnp.zeros((tb, LANES), F32), qh, True)
        carry, acc = lax.fori_loop(0, i, lambda n, ca: chunk(i - 1 - n, ca[0], ca[1], qh, False), (carry, acc))
        outs.append(acc)
    o_ref[0] = jnp.where(lane < HEAD_DIM, outs[0], outs[1]).astype(BF16)


def _sb_prompt(q, k, v, u):
    b, t, wd = q.shape
    assert t % SB_T == 0
    blk = lambda i, p, j: (i, j, p)
    per = lambda i, p, j: (i, 0, p)
    return pl.pallas_call(
        _sb_kernel,
        grid=(b, wd // LANES, t // SB_T),
        in_specs=[pl.BlockSpec((1, SB_T, LANES), blk), pl.BlockSpec((1, t, LANES), per),
                  pl.BlockSpec((1, t, LANES), per), pl.BlockSpec(u.shape, lambda i, p, j: (0, 0))],
        out_specs=pl.BlockSpec((1, SB_T, LANES), blk),
        out_shape=jax.ShapeDtypeStruct((b, t, wd), BF16),
        compiler_params=_cparams(("parallel", "parallel", "arbitrary")),
        name="sb_prompt",
    )(q, k, v, u)


def _mix_kernel(x_ref, oa_ref, ob_ref, gpre_ref, wmg_ref, woa_ref, wob_ref, wout_ref, gpost_ref, o_ref):
    x = x_ref[...]
    d = x.shape[1]
    h = _rms(x, gpre_ref[...]).astype(BF16)
    mg = _dot(h, wmg_ref[...])
    ya = _dot(oa_ref[...], woa_ref[...])
    yb = _dot(ob_ref[...], wob_ref[...])
    merged = _sigmoid(mg[:, 0:d]) * ya + _sigmoid(mg[:, d:2 * d]) * yb
    o_ref[...] = x + _rms(_dot(merged.astype(BF16), wout_ref[...]), gpost_ref[...])


def _mix(x2d, oa, ob, gpre, wmg, woa, wob, wout, gpost, tm):
    n, d = x2d.shape
    row = lambda i: (i, 0)
    const = lambda i: (0, 0)
    wspec = lambda w: pl.BlockSpec(w.shape, const, pipeline_mode=pl.Buffered(1))
    return pl.pallas_call(
        _mix_kernel,
        grid=(n // tm,),
        in_specs=[pl.BlockSpec((tm, d), row), pl.BlockSpec((tm, oa.shape[1]), row), pl.BlockSpec((tm, ob.shape[1]), row),
                  pl.BlockSpec((1, d), const), wspec(wmg), wspec(woa), wspec(wob), wspec(wout), pl.BlockSpec((1, d), const)],
        out_specs=pl.BlockSpec((tm, d), row),
        out_shape=jax.ShapeDtypeStruct((n, d), F32),
        compiler_params=_cparams(("parallel",)),
        name="mix",
    )(x2d, oa, ob, gpre, wmg, woa, wob, wout, gpost)


FFN_CHUNK = 1024


def _ffn_kernel(x_ref, p_ref, gpre_ref, wup_ref, wdown_ref, gpost_ref, wple_ref, wpg_ref, o_ref):
    x = x_ref[...]
    h = _rms(x, gpre_ref[...]).astype(BF16)
    f = jnp.zeros(x.shape, F32)
    for c in range(wup_ref.shape[1] // FFN_CHUNK):
        cs = slice(c * FFN_CHUNK, (c + 1) * FFN_CHUNK)
        up = jnp.maximum(_dot(h, wup_ref[:, cs]), 0.0)
        f = f + _dot((up * up).astype(BF16), wdown_ref[cs, :])
    x = x + _rms(f, gpost_ref[...])
    ple = _dot(p_ref[...].astype(BF16), wple_ref[...])
    o_ref[...] = x + ple * _sigmoid(_dot(x.astype(BF16), wpg_ref[...]))


def _ffn(x2d, p2d, gpre, wup, wdown, gpost, wple, wpg, tm):
    n, d = x2d.shape
    row = lambda i: (i, 0)
    const = lambda i: (0, 0)
    wspec = lambda w: pl.BlockSpec(w.shape, const, pipeline_mode=pl.Buffered(1))
    return pl.pallas_call(
        _ffn_kernel,
        grid=(n // tm,),
        in_specs=[pl.BlockSpec((tm, d), row), pl.BlockSpec((tm, p2d.shape[1]), row), pl.BlockSpec((1, d), const),
                  wspec(wup), wspec(wdown), pl.BlockSpec((1, d), const), wspec(wple), wspec(wpg)],
        out_specs=pl.BlockSpec((tm, d), row),
        out_shape=jax.ShapeDtypeStruct((n, d), F32),
        compiler_params=_cparams(("parallel",)),
        name="ffn",
    )(x2d, p2d, gpre, wup, wdown, gpost, wple, wpg)


def _nsa_dec_kernel(pt_ref, q_ref, gate_ref, new_ref, wnew_ref, state_ref, cache_ref,
                    wlo_ref, whi_ref, pelo_ref, pehi_ref, c2s_ref, expand_ref,
                    o_ref, win_ref, buf, sem, *, n_pages, past):
    s_id = pl.program_id(0)
    n_seq = pl.num_programs(0)
    slot = s_id % 2
    page_rows = cache_ref.shape[1]
    feat = 4 * NSA_KVW
    n2 = buf.shape[1]
    n_cmp = n2 - 1
    n_sel = past // L_SEL + 1
    w_buf = state_ref.shape[1]

    def page_copy(seq, p, sl):
        return pltpu.make_async_copy(cache_ref.at[pt_ref[seq, p]], buf.at[sl, pl.ds(p * page_rows, page_rows), :], sem.at[sl])

    def start_all(seq, sl):
        lax.fori_loop(0, n_pages, lambda p, c: (page_copy(seq, p, sl).start(), c)[1], 0)

    @pl.when(s_id == 0)
    def _():
        start_all(0, 0)

    @pl.when(s_id + 1 < n_seq)
    def _():
        start_all(s_id + 1, 1 - slot)

    lax.fori_loop(0, n_pages, lambda p, c: (page_copy(s_id, p, slot).wait(), c)[1], 0)

    ylo, yhi = _compress_rows(lambda l: buf[slot, :, l * feat:l * feat + 2 * NSA_KVW],
                              wlo_ref, whi_ref, pelo_ref, pehi_ref, n2)
    kc = (ylo + pltpu.roll(yhi, n2 - 1, 0)).astype(BF16)

    qf = q_ref[0].astype(F32)
    lane1 = lax.broadcasted_iota(jnp.int32, (1, LANES), 1)
    rows = []
    for hd in range(NSA_HEADS):
        g = hd // NSA_GROUP
        b = qf[:, (hd // 2) * LANES:(hd // 2 + 1) * LANES]
        if hd % 2 != g:
            b = pltpu.roll(b, HEAD_DIM, 1)
        rows.append(jnp.where((lane1 // HEAD_DIM) == g, b, 0.0))
    q8 = jnp.concatenate(rows, axis=0).astype(BF16)
    head_row = lax.broadcasted_iota(jnp.int32, (NSA_HEADS, 1), 0)
    tq = jnp.full((NSA_KV, 1), past, jnp.int32)

    s = _nt(q8, kc[:, 0:LANES])
    ncol = lax.broadcasted_iota(jnp.int32, (1, n2), 1)
    mc = (ncol * D_CMP + (L_CMP - 1) <= past) & (ncol < n_cmp)
    p = jnp.where(mc, _masked_softmax(s, mc), 0.0)
    o_cmp = _dot(p.astype(BF16), kc[:, LANES:2 * LANES])
    psum = jnp.concatenate([jnp.sum(p[g * NSA_GROUP:(g + 1) * NSA_GROUP], axis=0, keepdims=True)
                            for g in range(NSA_KV)], axis=0)
    imp = jnp.dot(psum, c2s_ref[...], preferred_element_type=F32, precision=lax.Precision.HIGHEST)

    sel = _select_blocks(imp, tq, n_sel)
    selrow = jnp.dot(jnp.where(sel, 1.0, 0.0), expand_ref[...], preferred_element_type=F32,
                     precision=lax.Precision.HIGHEST)
    sel8 = jnp.concatenate([selrow[g:g + 1] for g in range(NSA_KV) for _ in range(NSA_GROUP)], axis=0) > 0.5
    new = new_ref[0]
    s_tok = [jnp.where(sel8, _nt(q8, buf[slot, :, l * feat + 2 * NSA_KVW:l * feat + 3 * NSA_KVW].astype(BF16)), NEG_INF)
             for l in range(TOK_PER_ROW)]
    s_new = _row_dot(q8, new[:, 2 * NSA_KVW:3 * NSA_KVW])
    m = s_new
    for st in s_tok:
        m = jnp.maximum(m, jnp.max(st, axis=1, keepdims=True))
    p_new = jnp.exp(s_new - m)
    den = p_new
    acc = p_new.astype(BF16).astype(F32) * new[:, 3 * NSA_KVW:4 * NSA_KVW].astype(BF16).astype(F32)
    for l, st in enumerate(s_tok):
        pt = jnp.exp(st - m)
        den = den + jnp.sum(pt, axis=1, keepdims=True)
        acc = acc + _dot(pt.astype(BF16), buf[slot, :, l * feat + 3 * NSA_KVW:l * feat + 4 * NSA_KVW].astype(BF16))
    o_sel = acc / den

    st = state_ref[0]
    wnew = wnew_ref[0]
    s = _nt(q8, st[:, 0:LANES].astype(BF16))
    wpos = past - w_buf + lax.broadcasted_iota(jnp.int32, (1, w_buf), 1)
    mw = (wpos > past - WINDOW) & (wpos >= 0)
    s = jnp.where(mw, s, NEG_INF)
    s_new = _row_dot(q8, wnew[:, 0:LANES])
    m = jnp.maximum(s_new, jnp.max(s, axis=1, keepdims=True))
    p_new = jnp.exp(s_new - m)
    pw = jnp.exp(s - m)
    den = p_new + jnp.sum(pw, axis=1, keepdims=True)
    o_w = (_dot(pw.astype(BF16), st[:, LANES:2 * LANES].astype(BF16))
           + p_new.astype(BF16).astype(F32) * wnew[:, LANES:2 * LANES].astype(BF16).astype(F32)) / den

    gate = jnp.broadcast_to(_sigmoid(gate_ref[0]), (NSA_HEADS, LANES))
    lane8 = lax.broadcasted_iota(jnp.int32, (NSA_HEADS, LANES), 1)

    def gcol(br):
        return jnp.sum(jnp.where(lane8 == 3 * head_row + br, gate, 0.0), axis=1, keepdims=True)

    o = gcol(0) * o_cmp + gcol(1) * o_sel + gcol(2) * o_w
    placed = []
    for hd in range(NSA_HEADS):
        r = o[hd:hd + 1]
        if hd % 2 != hd // NSA_GROUP:
            r = pltpu.roll(r, HEAD_DIM, 1)
        placed.append(r)
    pairs = [jnp.where(lane1 < HEAD_DIM, placed[2 * p], placed[2 * p + 1]) for p in range(NSA_HEADS // 2)]
    o_ref[0] = jnp.concatenate(pairs, axis=1).astype(BF16)

    win_ref[0, pl.ds(0, w_buf - 1), :] = state_ref[0, pl.ds(1, w_buf - 1), :]
    win_ref[0, pl.ds(w_buf - 1, 1), :] = wnew


def _nsa_decode(page_table, q, gate, new, wnew, state, cache_rows, wlo, whi, pelo, pehi, c2s, expand, past):
    db = q.shape[0]
    n_pages = page_table.shape[1]
    page_rows = cache_rows.shape[1]
    n2 = n_pages * page_rows
    w_buf = state.shape[1]
    seq3 = lambda i, pt: (i, 0, 0)
    c3 = lambda i, pt: (0, 0, 0)
    c2 = lambda i, pt: (0, 0)
    grid_spec = pltpu.PrefetchScalarGridSpec(
        num_scalar_prefetch=1,
        grid=(db,),
        in_specs=[pl.BlockSpec((1, 1, NSA_QW), seq3), pl.BlockSpec((1, 1, LANES), seq3),
                  pl.BlockSpec((1, 1, 4 * NSA_KVW), seq3), pl.BlockSpec((1, 1, 2 * NSA_KVW), seq3),
                  pl.BlockSpec((1, w_buf, 2 * NSA_KVW), seq3), pl.BlockSpec(memory_space=pl.ANY),
                  pl.BlockSpec(wlo.shape, c3), pl.BlockSpec(whi.shape, c3),
                  pl.BlockSpec(pelo.shape, c3), pl.BlockSpec(pehi.shape, c3),
                  pl.BlockSpec(c2s.shape, c2), pl.BlockSpec(expand.shape, c2)],
        out_specs=[pl.BlockSpec((1, 1, NSA_QW), seq3), pl.BlockSpec((1, w_buf, 2 * NSA_KVW), seq3)],
        scratch_shapes=[pltpu.VMEM((2, n2, cache_rows.shape[2]), F32), pltpu.SemaphoreType.DMA((2,))],
    )
    return pl.pallas_call(
        functools.partial(_nsa_dec_kernel, n_pages=n_pages, past=past),
        grid_spec=grid_spec,
        out_shape=[jax.ShapeDtypeStruct((db, 1, NSA_QW), BF16), jax.ShapeDtypeStruct((db, w_buf, 2 * NSA_KVW), F32)],
        compiler_params=_cparams(("arbitrary",)),
        name="nsa_decode",
    )(page_table, q, gate, new, wnew, state, cache_rows, wlo, whi, pelo, pehi, c2s, expand)


SB_DEC_PAGES = 8


def _suffix_sums(x):
    r, n = x.shape
    lane = lax.broadcasted_iota(jnp.int32, (r, LANES), 1)
    run = jnp.zeros((r, 1), F32)
    outs = [None] * (n // LANES)
    for j in reversed(range(n // LANES)):
        xb = x[:, j * LANES:(j + 1) * LANES]
        inc = xb
        d = 1
        while d < LANES:
            inc = inc + jnp.where(lane + d < LANES, pltpu.roll(inc, LANES - d, 1), 0.0)
            d *= 2
        outs[j] = inc - xb + run
        run = run + inc[:, 0:1]
    return jnp.concatenate(outs, axis=1), run


def _sb_dec_kernel(pt_ref, q_ref, cache_ref, o_ref, buf, sem, acc_ref, carry_ref, *, pages):
    s_id = pl.program_id(0)
    c_id = pl.program_id(1)
    n_seq = pl.num_programs(0)
    n_chunk = pl.num_programs(1)
    step = s_id * n_chunk + c_id
    slot = step % 2
    page = cache_ref.shape[1]

    def page_copy(seq, ch, p, sl):
        src = cache_ref.at[pt_ref[seq, (n_chunk - 1 - ch) * pages + p]]
        return pltpu.make_async_copy(src, buf.at[sl, pl.ds(p * page, page), :], sem.at[sl])

    def start_all(seq, ch, sl):
        for p in range(pages):
            page_copy(seq, ch, p, sl).start()

    @pl.when(step == 0)
    def _():
        start_all(0, 0, 0)

    nxt = step + 1

    @pl.when(nxt < n_seq * n_chunk)
    def _():
        start_all(nxt // n_chunk, nxt % n_chunk, 1 - slot)

    for p in range(pages):
        page_copy(s_id, c_id, p, slot).wait()

    @pl.when(c_id == 0)
    def _():
        acc_ref[...] = jnp.zeros_like(acc_ref)
        carry_ref[...] = jnp.zeros_like(carry_ref)

    head_row = lax.broadcasted_iota(jnp.int32, (SB_HEADS, SB_W), 0)
    lane = lax.broadcasted_iota(jnp.int32, (SB_HEADS, SB_W), 1)
    own = (lane // HEAD_DIM) == head_row
    q = jnp.broadcast_to(q_ref[0].astype(F32), (SB_HEADS, SB_W))
    q8 = jnp.where(own, q, 0.0).astype(BF16)

    z = _nt(q8, buf[slot, :, 0:SB_W].astype(BF16))
    sp = _softplus(z)
    lm = -sp
    later, total = _suffix_sums(lm)
    a = jnp.exp(z - sp + later + carry_ref[...])
    acc_ref[...] += _dot(a.astype(BF16), buf[slot, :, SB_W:2 * SB_W].astype(BF16))
    carry_ref[...] += total

    @pl.when(c_id == n_chunk - 1)
    def _():
        o_ref[0] = jnp.sum(jnp.where(own, acc_ref[...], 0.0), axis=0, keepdims=True).astype(BF16)


def _sb_decode(page_table, q, cache_tok):
    db = q.shape[0]
    n_pages = page_table.shape[1]
    page = cache_tok.shape[1]
    pages = min(SB_DEC_PAGES, n_pages)
    assert n_pages % pages == 0
    seq3 = lambda i, c, pt: (i, 0, 0)
    grid_spec = pltpu.PrefetchScalarGridSpec(
        num_scalar_prefetch=1,
        grid=(db, n_pages // pages),
        in_specs=[pl.BlockSpec((1, 1, SB_W), seq3), pl.BlockSpec(memory_space=pl.ANY)],
        out_specs=pl.BlockSpec((1, 1, SB_W), seq3),
        scratch_shapes=[pltpu.VMEM((2, pages * page, 2 * SB_W), F32), pltpu.SemaphoreType.DMA((2,)),
                        pltpu.VMEM((SB_HEADS, SB_W), F32), pltpu.VMEM((SB_HEADS, 1), F32)],
    )
    return pl.pallas_call(
        functools.partial(_sb_dec_kernel, pages=pages),
        grid_spec=grid_spec,
        out_shape=jax.ShapeDtypeStruct((db, 1, SB_W), BF16),
        compiler_params=_cparams(("arbitrary", "arbitrary")),
        name="sb_decode",
    )(page_table, q, cache_tok)


def _rope_tables(pos):
    half = HEAD_DIM // 2
    inv = jnp.power(ROPE_THETA, -jnp.arange(half, dtype=F32) * (2.0 / HEAD_DIM))
    ang = pos.astype(F32)[:, None] * inv[None, :]
    cos, sin = jnp.cos(ang), jnp.sin(ang)
    reps = LANES // HEAD_DIM
    return jnp.tile(cos, (1, 2 * reps)), jnp.tile(jnp.concatenate([-sin, sin], axis=1), (1, reps))


def _cmp_to_sel(n_cmp, n_sel, rows, cols):
    s = np.arange(n_cmp)[:, None] * D_CMP
    b = np.arange(n_sel)[None, :] * L_SEL
    m = np.zeros((rows, cols), np.float32)
    m[:n_cmp, :n_sel] = (s < b + L_SEL) & (s + L_CMP > b)
    return jnp.asarray(m)


def _round_up(a, b):
    return -(-a // b) * b


def _row_tile(n):
    for tm in (512, 256, 128, 64, 32, 16, 8):
        if n % tm == 0:
            return tm
    raise ValueError(f"row count {n} is not a multiple of 8")


def kernel(x_prompt, x_sample, p_prompt, p_sample, cache_nsa, cache_sb, state_win, page_table, w_in, w_cmp, pe_cmp,
           w_oa, w_ob, w_out, g_pre_mix, g_post_mix, g_pre_ffn, g_post_ffn, w_up, w_down, w_ple, w_ple_gate):
    B, T, D = x_prompt.shape
    DB, TS = x_sample.shape[:2]
    assert TS == 1, "the decode kernels take one new token per sequence"
    depth = w_in.shape[0]
    n_phys, page = cache_nsa.shape[1:3]
    n_pages = page_table.shape[1]
    past = n_pages * page
    w_buf = state_win.shape[2]
    assert T % TOK_PER_ROW == 0 and page % TOK_PER_ROW == 0 and w_buf >= 2
    n2_p = T // TOK_PER_ROW
    n2_s = past // TOK_PER_ROW
    feat = 4 * NSA_KVW

    cos_p, sin_p = _rope_tables(jnp.arange(T, dtype=jnp.int32))
    cos_s, sin_s = _rope_tables(jnp.full((DB,), past, jnp.int32))
    wsel_p = _round_up(T // L_SEL, LANES)
    c2s_p = _cmp_to_sel(n2_p - 1, T // L_SEL, n2_p, wsel_p)
    n_sel_s = past // L_SEL + 1
    wsel_s = _round_up(n_sel_s, LANES)
    c2s_s = _cmp_to_sel(n2_s - 1, n_sel_s, n2_s, wsel_s)
    rows_per_blk = L_SEL // TOK_PER_ROW
    expand = jnp.asarray((np.arange(n2_s)[None, :] // rows_per_blk == np.arange(wsel_s)[:, None]).astype(np.float32))
    et = jnp.asarray(np.where(np.arange(T)[:, None] // L_SEL == np.arange(wsel_p)[None, :], NEG_INF, 0.0), BF16)
    tri = jnp.asarray(np.arange(SB_T)[:, None] > np.arange(SB_T)[None, :], BF16)

    tm_p = _row_tile(B * T)
    tm_s = _row_tile(DB)
    xp = x_prompt.reshape(B * T, D)
    xs = x_sample.reshape(DB, D)
    o = np.cumsum((0, NSA_QW, 6 * NSA_KVW, GATE_W, 3 * SB_W, 2 * D))
    nsa_p, sb_p, win_p, nsa_s, sb_s, win_s = [], [], [], [], [], []
    for i in range(depth):
        w = w_in[i]
        w_proj = jnp.concatenate([w[:, o[0]:o[2]], w[:, o[3]:o[4]],
                                  jnp.pad(w[:, o[2]:o[3]], ((0, 0), (0, LANES - GATE_W)))], axis=1).astype(BF16)
        w_mg = w[:, o[4]:o[5]].astype(BF16)
        wc = jnp.repeat(w_cmp[i], NSA_KV, axis=0)
        wblk = jnp.einsum('ab,alde->ladbe', jnp.eye(2 * NSA_KV, dtype=F32), wc).reshape(L_CMP, 2 * NSA_KVW, 2 * NSA_KVW).astype(BF16)
        wlo, whi = wblk[:TOK_PER_ROW], wblk[TOK_PER_ROW:]
        pe = jnp.broadcast_to(pe_cmp[i][:, :, None, :], (L_CMP, 2, NSA_KV, HEAD_DIM)).reshape(L_CMP, 1, 2 * NSA_KVW)
        pelo, pehi = pe[:TOK_PER_ROW], pe[TOK_PER_ROW:]
        g_pre = g_pre_mix[i].reshape(1, D)
        tail_w = (g_pre, w_mg, w_oa[i].astype(BF16), w_ob[i].astype(BF16), w_out[i].astype(BF16), g_post_mix[i].reshape(1, D))
        ffn_w = (g_pre_ffn[i].reshape(1, D), w_up[i].astype(BF16), w_down[i].astype(BF16), g_post_ffn[i].reshape(1, D),
                 w_ple[i].astype(BF16), w_ple_gate[i].astype(BF16))

        (q, nsa, win, gate, sbq, sbkv, ksel, vsel, kwin, vwin, sbk, sbv) = _project(xp, g_pre, w_proj, cos_p, sin_p, tm_p)
        b3 = lambda a: a.reshape(B, T, a.shape[-1])
        kc = _compress(nsa.reshape(B, n2_p, TOK_PER_ROW * feat), wlo, whi, pelo, pehi, min(128, n2_p))
        o_a = _nsa_prompt(b3(q), b3(gate), kc, b3(ksel), b3(vsel), b3(kwin), b3(vwin), c2s_p, et)
        o_b = _sb_prompt(b3(sbq), b3(sbk), b3(sbv), tri)
        xp = _mix(xp, o_a.reshape(B * T, NSA_QW), o_b.reshape(B * T, SB_W), *tail_w, tm_p)
        xp = _ffn(xp, p_prompt[i].reshape(B * T, -1), *ffn_w, tm_p)
        nsa_p.append(nsa.reshape(B, T, 4, NSA_KV, HEAD_DIM))
        sb_p.append(sbkv.reshape(B, T, 2, SB_HEADS, HEAD_DIM))
        win_p.append(b3(win)[:, T - min(WINDOW, T):].reshape(B, -1, 2, NSA_KV, HEAD_DIM))

        (q, nsa, win, gate, sbq, sbkv, _, _, _, _, _, _) = _project(xs, g_pre, w_proj, cos_s, sin_s, tm_s)
        r3 = lambda a: a.reshape(DB, 1, a.shape[-1])
        o_a, win_new = _nsa_decode(page_table, r3(q), r3(gate), r3(nsa), r3(win),
                                   state_win[i].reshape(DB, w_buf, 2 * NSA_KVW),
                                   cache_nsa[i].reshape(n_phys, page // TOK_PER_ROW, TOK_PER_ROW * feat),
                                   wlo, whi, pelo, pehi, c2s_s, expand, past)
        o_b = _sb_decode(page_table, r3(sbq), cache_sb[i].reshape(n_phys, page, 2 * SB_W))
        xs = _mix(xs, o_a.reshape(DB, NSA_QW), o_b.reshape(DB, SB_W), *tail_w, tm_s)
        xs = _ffn(xs, p_sample[i].reshape(DB, -1), *ffn_w, tm_s)
        nsa_s.append(nsa.reshape(DB, TS, 4, NSA_KV, HEAD_DIM))
        sb_s.append(sbkv.reshape(DB, TS, 2, SB_HEADS, HEAD_DIM))
        win_s.append(win_new.reshape(DB, w_buf, 2, NSA_KV, HEAD_DIM))
    return (xp.reshape(B, T, D), xs.reshape(DB, TS, D), jnp.stack(nsa_p), jnp.stack(sb_p), jnp.stack(win_p),
            jnp.stack(nsa_s), jnp.stack(sb_s), jnp.stack(win_s))
```

```python
import functools

import numpy as np
import jax
import jax.numpy as jnp
from jax import lax
from jax.experimental import pallas as pl
from jax.experimental.pallas import tpu as pltpu

F32 = jnp.float32
BF16 = jnp.bfloat16

HEAD_DIM = 64
NSA_HEADS = 8
NSA_KV = 2
NSA_GROUP = NSA_HEADS // NSA_KV
SB_HEADS = 8
L_CMP = 32
D_CMP = 16
L_SEL = 64
N_SEL = 16
N_LOCAL = 2
WINDOW = 512
ROPE_THETA = 10000.0
EPS = 1e-6
FORCE_BONUS = 1e4
NEG_INF = -1e30
LOWEST = -3e38
SCALE = HEAD_DIM ** -0.5
LOG2E = 1.4426950408889634
NSA_QW = NSA_HEADS * HEAD_DIM
NSA_KVW = NSA_KV * HEAD_DIM
SB_W = SB_HEADS * HEAD_DIM
GATE_W = 3 * NSA_HEADS

LANES = 128
TOK_PER_ROW = 16
VMEM_LIMIT = 56 * 1024 * 1024


def _cparams(sem):
    return pltpu.CompilerParams(dimension_semantics=sem, vmem_limit_bytes=VMEM_LIMIT)


def _dot(a, b):
    return jnp.dot(a, b, preferred_element_type=F32)


def _nt(a, b):
    return lax.dot_general(a, b, (((1,), (1,)), ((), ())), preferred_element_type=F32)


def _row_dot(a, row):
    return jnp.sum(a.astype(F32) * row.astype(BF16).astype(F32), axis=1, keepdims=True)


def _rms(x, g):
    return x * lax.rsqrt(jnp.mean(x * x, axis=-1, keepdims=True) + EPS) * g


def _sigmoid(x):
    return 1.0 / (1.0 + jnp.exp(-x))


def _masked_softmax(s, mask):
    s = jnp.where(mask, s, NEG_INF)
    e = jnp.exp(s - jnp.max(s, axis=-1, keepdims=True))
    return e / jnp.sum(e, axis=-1, keepdims=True)


def _proj_kernel(x_ref, g_ref, w_ref, cos_ref, sin_ref,
                 q_ref, nsa_ref, win_ref, gate_ref, sbq_ref, sbkv_ref,
                 ksel_ref, vsel_ref, kwin_ref, vwin_ref, sbk_ref, sbv_ref):
    h = _rms(x_ref[...], g_ref[...]).astype(BF16)
    cos = cos_ref[...]
    sin = sin_ref[...]
    lane = lax.broadcasted_iota(jnp.int32, cos.shape, 1)
    first_half = (lane & (HEAD_DIM - 1)) < HEAD_DIM // 2

    def rope(z):
        other = jnp.where(first_half, pltpu.roll(z, LANES - HEAD_DIM // 2, 1), pltpu.roll(z, HEAD_DIM // 2, 1))
        return z * cos + other * sin

    def blk(z, c):
        return z[:, c * LANES:(c + 1) * LANES]

    zq = _dot(h, w_ref[:, 0:NSA_QW])
    q = jnp.concatenate([rope(blk(zq, c)) for c in range(NSA_QW // LANES)], axis=1)
    q_ref[...] = (q * SCALE).astype(BF16)

    o = NSA_QW
    zkv = _dot(h, w_ref[:, o:o + 6 * NSA_KVW])
    kc, vc, ks, vs, kw, vw = rope(blk(zkv, 0)), blk(zkv, 1), rope(blk(zkv, 2)), blk(zkv, 3), rope(blk(zkv, 4)), blk(zkv, 5)
    nsa_ref[...] = jnp.concatenate([kc, vc, ks, vs], axis=1)
    win_ref[...] = jnp.concatenate([kw, vw], axis=1)
    ksel_ref[...] = ks.astype(BF16)
    vsel_ref[...] = vs.astype(BF16)
    kwin_ref[...] = kw.astype(BF16)
    vwin_ref[...] = vw.astype(BF16)

    o += 6 * NSA_KVW
    zsb = _dot(h, w_ref[:, o:o + 3 * SB_W])
    sbq_ref[...] = (zsb[:, 0:SB_W] * SCALE).astype(BF16)
    sbkv_ref[...] = zsb[:, SB_W:3 * SB_W]
    sbk_ref[...] = zsb[:, SB_W:2 * SB_W].astype(BF16)
    sbv_ref[...] = zsb[:, 2 * SB_W:3 * SB_W].astype(BF16)

    o += 3 * SB_W
    gate_ref[...] = _dot(h, w_ref[:, o:o + LANES])


def _project(x2d, g, w_proj, cos, sin, tm):
    n, d = x2d.shape
    n_tab = cos.shape[0] // tm
    row = lambda i: (i, 0)
    const = lambda i: (0, 0)
    tab = lambda i: (i % n_tab, 0)
    widths = [(NSA_QW, BF16), (4 * NSA_KVW, F32), (2 * NSA_KVW, F32), (LANES, F32), (SB_W, BF16), (2 * SB_W, F32),
              (NSA_KVW, BF16), (NSA_KVW, BF16), (NSA_KVW, BF16), (NSA_KVW, BF16), (SB_W, BF16), (SB_W, BF16)]
    return pl.pallas_call(
        _proj_kernel,
        grid=(n // tm,),
        in_specs=[pl.BlockSpec((tm, d), row), pl.BlockSpec((1, d), const),
                  pl.BlockSpec(w_proj.shape, const, pipeline_mode=pl.Buffered(1)),
                  pl.BlockSpec((tm, LANES), tab), pl.BlockSpec((tm, LANES), tab)],
        out_specs=[pl.BlockSpec((tm, wd), row) for wd, _ in widths],
        out_shape=[jax.ShapeDtypeStruct((n, wd), dt) for wd, dt in widths],
        compiler_params=_cparams(("parallel",)),
        name="proj",
    )(x2d, g, w_proj, cos, sin)


def _compress_rows(x_of, wlo_ref, whi_ref, pelo_ref, pehi_ref, rows):
    ylo = jnp.zeros((rows, 2 * NSA_KVW), F32)
    yhi = jnp.zeros((rows, 2 * NSA_KVW), F32)
    for l in range(TOK_PER_ROW):
        xl = x_of(l)
        ylo = ylo + _dot((xl + pelo_ref[l]).astype(BF16), wlo_ref[l])
        yhi = yhi + _dot((xl + pehi_ref[l]).astype(BF16), whi_ref[l])
    return ylo, yhi


def _compress_kernel(x_ref, wlo_ref, whi_ref, pelo_ref, pehi_ref, out_ref, ylo_ref, yhi_ref):
    t = pl.program_id(1)
    rows = x_ref.shape[1]
    n2 = ylo_ref.shape[0]
    feat = 4 * NSA_KVW
    ylo, yhi = _compress_rows(lambda l: x_ref[0, :, l * feat:l * feat + 2 * NSA_KVW],
                              wlo_ref, whi_ref, pelo_ref, pehi_ref, rows)
    r0 = pl.multiple_of(t * rows, rows)
    ylo_ref[pl.ds(r0, rows), :] = ylo
    yhi_ref[pl.ds(r0, rows), :] = yhi

    @pl.when(t == pl.num_programs(1) - 1)
    def _():
        out_ref[0] = (ylo_ref[...] + pltpu.roll(yhi_ref[...], n2 - 1, 0)).astype(BF16)


def _compress(nsa_rows, wlo, whi, pelo, pehi, rows):
    b, n2, wd = nsa_rows.shape
    c3 = lambda i, t: (0, 0, 0)
    return pl.pallas_call(
        _compress_kernel,
        grid=(b, n2 // rows),
        in_specs=[pl.BlockSpec((1, rows, wd), lambda i, t: (i, t, 0)),
                  pl.BlockSpec(wlo.shape, c3), pl.BlockSpec(whi.shape, c3),
                  pl.BlockSpec(pelo.shape, c3), pl.BlockSpec(pehi.shape, c3)],
        out_specs=pl.BlockSpec((1, n2, 2 * NSA_KVW), lambda i, t: (i, 0, 0)),
        out_shape=jax.ShapeDtypeStruct((b, n2, 2 * NSA_KVW), BF16),
        scratch_shapes=[pltpu.VMEM((n2, 2 * NSA_KVW), F32), pltpu.VMEM((n2, 2 * NSA_KVW), F32)],
        compiler_params=_cparams(("parallel", "arbitrary")),
        name="compress",
    )(nsa_rows, wlo, whi, pelo, pehi)


def _select_blocks(imp, tq, n_sel):
    w = imp.shape[1]
    blk = lax.broadcasted_iota(jnp.int32, (1, w), 1)
    cur = tq // L_SEL
    forced = (blk == 0) | ((blk <= cur) & (blk > cur - N_LOCAL))
    valid = (blk * L_SEL <= tq) & (blk < n_sel)
    x = jnp.where(valid, jnp.where(forced, imp + FORCE_BONUS, imp), NEG_INF)
    x = jnp.where(blk < n_sel, x, LOWEST)
    sel = jnp.zeros(x.shape, jnp.bool_)
    for _ in range(min(N_SEL, n_sel)):
        m = jnp.max(x, axis=1, keepdims=True)
        first = jnp.min(jnp.where(x == m, blk, w), axis=1, keepdims=True)
        hit = blk == first
        sel = sel | hit
        x = jnp.where(hit, LOWEST, x)
    return sel & valid


def _select_blocks_t(imp_t, tq_row, n_sel):
    w = imp_t.shape[0]
    blk = lax.broadcasted_iota(jnp.int32, (w, 1), 0)
    cur = tq_row // L_SEL
    forced = (blk == 0) | ((blk <= cur) & (blk > cur - N_LOCAL))
    valid = (blk * L_SEL <= tq_row) & (blk < n_sel)
    x = jnp.where(valid, jnp.where(forced, imp_t + FORCE_BONUS, imp_t), NEG_INF)
    x = jnp.where(blk < n_sel, x, LOWEST)
    for _ in range(min(N_SEL, n_sel)):
        m = jnp.max(x, axis=0, keepdims=True)
        first = jnp.min(jnp.where(x == m, blk, w), axis=0, keepdims=True)
        x = jnp.where(blk == first, LOWEST, x)
    return (x == LOWEST) & valid


NSA_TQ = 128
NSA_KC = 512


def _nsa_kernel(q_ref, gate_ref, kc_ref, ksel_ref, vsel_ref, kwin_ref, vwin_ref, c2s_ref, et_ref, o_ref, *, seq):
    i = pl.program_id(1)
    tq_n = NSA_TQ
    q0 = i * tq_n
    n2 = kc_ref.shape[1]
    n_cmp = n2 - 1
    n_sel = seq // L_SEL
    wsel = c2s_ref.shape[1]
    rows = NSA_HEADS * tq_n

    qf = q_ref[0].astype(F32)
    gate = _sigmoid(gate_ref[0])
    lane = lax.broadcasted_iota(jnp.int32, (tq_n, LANES), 1)
    tq = q0 + lax.broadcasted_iota(jnp.int32, (tq_n, 1), 0)
    t8 = jnp.concatenate([tq] * NSA_HEADS, axis=0)
    tq_row = q0 + lax.broadcasted_iota(jnp.int32, (1, tq_n), 1)
    blkid = lax.broadcasted_iota(jnp.int32, (1, wsel), 1)

    def per_head(by_group):
        return jnp.concatenate([by_group[hd // NSA_GROUP] for hd in range(NSA_HEADS)], axis=0)

    def online(carry, s, v):
        m, l, acc = carry
        mn = jnp.maximum(m, jnp.max(s, axis=1, keepdims=True))
        a = jnp.exp(m - mn)
        p = jnp.exp(s - mn)
        return mn, a * l + jnp.sum(p, axis=1, keepdims=True), a * acc + _dot(p.astype(BF16), v)

    embs = []
    for hd in range(NSA_HEADS):
        g = hd // NSA_GROUP
        b = qf[:, (hd // 2) * LANES:(hd // 2 + 1) * LANES]
        if hd % 2 != g:
            b = pltpu.roll(b, HEAD_DIM, 1)
        embs.append(jnp.where((lane // HEAD_DIM) == g, b, 0.0))
    q8 = jnp.concatenate(embs, axis=0).astype(BF16)

    s = _nt(q8, kc_ref[0, :, 0:LANES])
    ncol = lax.broadcasted_iota(jnp.int32, (1, n2), 1)
    mc = (ncol * D_CMP + (L_CMP - 1) <= t8) & (ncol < n_cmp)
    p = jnp.where(mc, _masked_softmax(s, mc), 0.0)
    o_cmp = _dot(p.astype(BF16), kc_ref[0, :, LANES:2 * LANES])
    psums = []
    for g in range(NSA_KV):
        blocks = [p[hd * tq_n:(hd + 1) * tq_n] for hd in range(g * NSA_GROUP, (g + 1) * NSA_GROUP)]
        psums.append(functools.reduce(lambda a, b: a + b, blocks))
    imp = jnp.dot(jnp.concatenate(psums, axis=0), c2s_ref[...], preferred_element_type=F32,
                  precision=lax.Precision.HIGHEST)

    sel_t = _select_blocks_t(imp.T, jnp.concatenate([tq_row] * NSA_KV, axis=1), n_sel)
    notsel = jnp.where(sel_t, 0.0, 1.0).T
    notsel_g = [notsel[g * tq_n:(g + 1) * tq_n] for g in range(NSA_KV)]
    before_diag = [jnp.where(blkid >= q0 // L_SEL, 1.0, ns) for ns in notsel_g]
    aug_main = jnp.concatenate([q8, per_head(before_diag).astype(BF16)], axis=1)
    aug_diag = jnp.concatenate([q8, per_head(notsel_g).astype(BF16)], axis=1)

    def main_body(c, carry):
        k0 = pl.multiple_of(c * NSA_KC, NSA_KC)
        rhs = jnp.concatenate([ksel_ref[0, pl.ds(k0, NSA_KC), :], et_ref[pl.ds(k0, NSA_KC), :]], axis=1)
        return online(carry, _nt(aug_main, rhs), vsel_ref[0, pl.ds(k0, NSA_KC), :])

    init = (jnp.full((rows, 1), LOWEST, F32), jnp.zeros((rows, 1), F32), jnp.zeros((rows, LANES), F32))
    carry = lax.fori_loop(0, (q0 + NSA_KC - 1) // NSA_KC, main_body, init)
    kd = pl.multiple_of(q0, tq_n)
    rhs = jnp.concatenate([ksel_ref[0, pl.ds(kd, tq_n), :], et_ref[pl.ds(kd, tq_n), :]], axis=1)
    s = jnp.where(tq_row <= t8, _nt(aug_diag, rhs), NEG_INF)
    _, l, acc = online(carry, s, vsel_ref[0, pl.ds(kd, tq_n), :])
    o_sel = acc / l

    wlen = WINDOW + tq_n
    w0 = pl.multiple_of(jnp.maximum(q0 - WINDOW, 0), tq_n)
    wpos = w0 + lax.broadcasted_iota(jnp.int32, (1, wlen), 1)
    mw = (wpos <= t8) & (wpos > t8 - WINDOW)
    s = _nt(q8, kwin_ref[0, pl.ds(w0, wlen), :])
    o_w = _dot(_masked_softmax(s, mw).astype(BF16), vwin_ref[0, pl.ds(w0, wlen), :])

    outs = []
    for hd in range(NSA_HEADS):
        r = slice(hd * tq_n, (hd + 1) * tq_n)
        o = (gate[:, 3 * hd:3 * hd + 1] * o_cmp[r] + gate[:, 3 * hd + 1:3 * hd + 2] * o_sel[r]
             + gate[:, 3 * hd + 2:3 * hd + 3] * o_w[r])
        if hd % 2 != hd // NSA_GROUP:
            o = pltpu.roll(o, HEAD_DIM, 1)
        outs.append(o)
    pairs = [jnp.where(lane < HEAD_DIM, outs[2 * p], outs[2 * p + 1]) for p in range(NSA_HEADS // 2)]
    o_ref[0] = jnp.concatenate(pairs, axis=1).astype(BF16)


def _nsa_prompt(q, gate, kc, ksel, vsel, kwin, vwin, c2s, et):
    b, t, _ = q.shape
    assert t % NSA_KC == 0 and t >= WINDOW + NSA_TQ
    blk = lambda i, j: (i, j, 0)
    per_b = lambda i, j: (i, 0, 0)
    c2 = lambda i, j: (0, 0)
    kv_spec = pl.BlockSpec((1, t, LANES), per_b)
    return pl.pallas_call(
        functools.partial(_nsa_kernel, seq=t),
        grid=(b, t // NSA_TQ),
        in_specs=[pl.BlockSpec((1, NSA_TQ, NSA_QW), blk), pl.BlockSpec((1, NSA_TQ, LANES), blk),
                  pl.BlockSpec((1,) + kc.shape[1:], per_b), kv_spec, kv_spec, kv_spec, kv_spec,
                  pl.BlockSpec(c2s.shape, c2), pl.BlockSpec(et.shape, c2)],
        out_specs=pl.BlockSpec((1, NSA_TQ, NSA_QW), blk),
        out_shape=jax.ShapeDtypeStruct((b, t, NSA_QW), BF16),
        compiler_params=_cparams(("parallel", "arbitrary")),
        name="nsa_prompt",
    )(q, gate, kc, ksel, vsel, kwin, vwin, c2s, et)


SB_T = 256


def _softplus2(z2):
    return jnp.maximum(z2, 0.0) + jnp.log2(1.0 + jnp.exp2(-jnp.abs(z2)))


def _sb_kernel(q_ref, k_ref, v_ref, u_ref, o_ref):
    i = pl.program_id(2)
    tb = SB_T
    q = q_ref[0].astype(F32)
    lane = lax.broadcasted_iota(jnp.int32, (tb, LANES), 1)
    u = u_ref[...]
    qpos = lax.broadcasted_iota(jnp.int32, (tb, 1), 0)
    kpos = lax.broadcasted_iota(jnp.int32, (1, tb), 1)
    below = jnp.concatenate([kpos < qpos] * 2, axis=0)
    q2 = jnp.concatenate([jnp.where((lane // HEAD_DIM) == hh, q, 0.0) for hh in range(2)], axis=0).astype(BF16)

    def scores(c):
        k0 = pl.multiple_of(c * tb, tb)
        return _nt(q2, k_ref[0, pl.ds(k0, tb), :]) * LOG2E

    def chunk(c, z, carry, acc, diagonal):
        sp = _softplus2(z)
        lm = -sp
        if diagonal:
            lm = jnp.where(below, lm, 0.0)
        hi = lm.astype(BF16)
        lo = (lm - hi.astype(F32)).astype(BF16)
        parts = _dot(jnp.concatenate([hi, lo], axis=0), u)
        after = parts[0:2 * tb] + parts[2 * tb:4 * tb] + carry
        a = jnp.exp2(z - sp + after)
        if diagonal:
            a = jnp.where(below, a, 0.0)
        k0 = pl.multiple_of(c * tb, tb)
        acc = acc + _dot(a.astype(BF16), v_ref[0, pl.ds(k0, tb), :])
        return carry + jnp.sum(lm, axis=1, keepdims=True), acc

    def body(n, state):
        z, carry, acc = state
        c = i - 1 - n
        z_next = scores(jnp.maximum(c - 1, 0))
        carry, acc = chunk(c, z, carry, acc, False)
        return z_next, carry, acc

    carry, acc = chunk(i, scores(i), jnp.zeros((2 * tb, 1), F32), jnp.zeros((2 * tb, LANES), F32), True)
    _, _, acc = lax.fori_loop(0, i, body, (scores(jnp.maximum(i - 1, 0)), carry, acc))
    o_ref[0] = jnp.where(lane < HEAD_DIM, acc[0:tb], acc[tb:2 * tb]).astype(BF16)


def _sb_prompt(q, k, v, u):
    b, t, wd = q.shape
    assert t % SB_T == 0
    blk = lambda i, p, j: (i, j, p)
    per = lambda i, p, j: (i, 0, p)
    return pl.pallas_call(
        _sb_kernel,
        grid=(b, wd // LANES, t // SB_T),
        in_specs=[pl.BlockSpec((1, SB_T, LANES), blk), pl.BlockSpec((1, t, LANES), per),
                  pl.BlockSpec((1, t, LANES), per), pl.BlockSpec(u.shape, lambda i, p, j: (0, 0))],
        out_specs=pl.BlockSpec((1, SB_T, LANES), blk),
        out_shape=jax.ShapeDtypeStruct((b, t, wd), BF16),
        compiler_params=_cparams(("parallel", "parallel", "arbitrary")),
        name="sb_prompt",
    )(q, k, v, u)


def _mix_kernel(x_ref, oa_ref, ob_ref, gpre_ref, wmg_ref, woa_ref, wob_ref, wout_ref, gpost_ref, o_ref):
    x = x_ref[...]
    d = x.shape[1]
    h = _rms(x, gpre_ref[...]).astype(BF16)
    mg = _dot(h, wmg_ref[...])
    ya = _dot(oa_ref[...], woa_ref[...])
    yb = _dot(ob_ref[...], wob_ref[...])
    merged = _sigmoid(mg[:, 0:d]) * ya + _sigmoid(mg[:, d:2 * d]) * yb
    o_ref[...] = x + _rms(_dot(merged.astype(BF16), wout_ref[...]), gpost_ref[...])


def _mix(x2d, oa, ob, gpre, wmg, woa, wob, wout, gpost, tm):
    n, d = x2d.shape
    row = lambda i: (i, 0)
    const = lambda i: (0, 0)
    wspec = lambda w: pl.BlockSpec(w.shape, const, pipeline_mode=pl.Buffered(1))
    return pl.pallas_call(
        _mix_kernel,
        grid=(n // tm,),
        in_specs=[pl.BlockSpec((tm, d), row), pl.BlockSpec((tm, oa.shape[1]), row), pl.BlockSpec((tm, ob.shape[1]), row),
                  pl.BlockSpec((1, d), const), wspec(wmg), wspec(woa), wspec(wob), wspec(wout), pl.BlockSpec((1, d), const)],
        out_specs=pl.BlockSpec((tm, d), row),
        out_shape=jax.ShapeDtypeStruct((n, d), F32),
        compiler_params=_cparams(("parallel",)),
        name="mix",
    )(x2d, oa, ob, gpre, wmg, woa, wob, wout, gpost)


FFN_CHUNK = 1024


def _ffn_kernel(x_ref, p_ref, gpre_ref, wup_ref, wdown_ref, gpost_ref, wple_ref, wpg_ref, o_ref):
    x = x_ref[...]
    h = _rms(x, gpre_ref[...]).astype(BF16)
    f = jnp.zeros(x.shape, F32)
    for c in range(wup_ref.shape[1] // FFN_CHUNK):
        cs = slice(c * FFN_CHUNK, (c + 1) * FFN_CHUNK)
        up = jnp.maximum(_dot(h, wup_ref[:, cs]), 0.0)
        f = f + _dot((up * up).astype(BF16), wdown_ref[cs, :])
    x = x + _rms(f, gpost_ref[...])
    ple = _dot(p_ref[...].astype(BF16), wple_ref[...])
    o_ref[...] = x + ple * _sigmoid(_dot(x.astype(BF16), wpg_ref[...]))


def _ffn(x2d, p2d, gpre, wup, wdown, gpost, wple, wpg, tm):
    n, d = x2d.shape
    row = lambda i: (i, 0)
    const = lambda i: (0, 0)
    wspec = lambda w: pl.BlockSpec(w.shape, const, pipeline_mode=pl.Buffered(1))
    return pl.pallas_call(
        _ffn_kernel,
        grid=(n // tm,),
        in_specs=[pl.BlockSpec((tm, d), row), pl.BlockSpec((tm, p2d.shape[1]), row), pl.BlockSpec((1, d), const),
                  wspec(wup), wspec(wdown), pl.BlockSpec((1, d), const), wspec(wple), wspec(wpg)],
        out_specs=pl.BlockSpec((tm, d), row),
        out_shape=jax.ShapeDtypeStruct((n, d), F32),
        compiler_params=_cparams(("parallel",)),
        name="ffn",
    )(x2d, p2d, gpre, wup, wdown, gpost, wple, wpg)


def _nsa_dec_kernel(pt_ref, q_ref, gate_ref, new_ref, wnew_ref, state_ref, cache_ref,
                    wlo_ref, whi_ref, pelo_ref, pehi_ref, c2s_ref,
                    o_ref, win_ref, buf, sem, tok_ref, *, n_pages, past):
    s_id = pl.program_id(0)
    n_seq = pl.num_programs(0)
    slot = s_id % 2
    page = buf.shape[3]
    n2 = tok_ref.shape[1] // TOK_PER_ROW
    n_cmp = n2 - 1
    n_sel = past // L_SEL + 1
    w_buf = state_ref.shape[2]
    cmp_w = 2 * NSA_KVW

    def page_copy(seq, p, sl):
        return pltpu.make_async_copy(cache_ref.at[pt_ref[seq, p]], buf.at[sl, p], sem.at[sl])

    def start_all(seq, sl):
        lax.fori_loop(0, n_pages, lambda p, c: (page_copy(seq, p, sl).start(), c)[1], 0)

    @pl.when(s_id == 0)
    def _():
        start_all(0, 0)

    @pl.when(s_id + 1 < n_seq)
    def _():
        start_all(s_id + 1, 1 - slot)

    lax.fori_loop(0, n_pages, lambda p, c: (page_copy(s_id, p, slot).wait(), c)[1], 0)

    halves = range(cmp_w // LANES)

    def to_token_major(p, c):
        r0 = pl.multiple_of(p * page, page)
        for h in halves:
            tok_ref[h, pl.ds(r0, page), :] = buf[slot, p, h * LANES:(h + 1) * LANES, :].T
        return c

    lax.fori_loop(0, n_pages, to_token_major, 0)
    ylo, yhi = _compress_rows(
        lambda l: jnp.concatenate([tok_ref[h, pl.ds(l, n2, stride=TOK_PER_ROW), :] for h in halves], axis=1),
        wlo_ref, whi_ref, pelo_ref, pehi_ref, n2)
    kc = (ylo + pltpu.roll(yhi, n2 - 1, 0)).astype(BF16)

    qf = q_ref[0].astype(F32)
    lane1 = lax.broadcasted_iota(jnp.int32, (1, LANES), 1)
    rows = []
    for hd in range(NSA_HEADS):
        g = hd // NSA_GROUP
        b = qf[:, (hd // 2) * LANES:(hd // 2 + 1) * LANES]
        if hd % 2 != g:
            b = pltpu.roll(b, HEAD_DIM, 1)
        rows.append(jnp.where((lane1 // HEAD_DIM) == g, b, 0.0))
    q8 = jnp.concatenate(rows, axis=0).astype(BF16)
    head_row = lax.broadcasted_iota(jnp.int32, (NSA_HEADS, 1), 0)
    tq = jnp.full((NSA_KV, 1), past, jnp.int32)

    s = _nt(q8, kc[:, 0:LANES])
    ncol = lax.broadcasted_iota(jnp.int32, (1, n2), 1)
    mc = (ncol * D_CMP + (L_CMP - 1) <= past) & (ncol < n_cmp)
    p = jnp.where(mc, _masked_softmax(s, mc), 0.0)
    o_cmp = _dot(p.astype(BF16), kc[:, LANES:2 * LANES])
    psum = jnp.concatenate([jnp.sum(p[g * NSA_GROUP:(g + 1) * NSA_GROUP], axis=0, keepdims=True)
                            for g in range(NSA_KV)], axis=0)
    imp = jnp.dot(psum, c2s_ref[...], preferred_element_type=F32, precision=lax.Precision.HIGHEST)

    sel = jnp.where(_select_blocks(imp, tq, n_sel), 1.0, 0.0)
    lane8 = lax.broadcasted_iota(jnp.int32, (NSA_HEADS, LANES), 1)
    blocks_per_page = page // L_SEL
    page_tok = lax.broadcasted_iota(jnp.int32, (1, page), 1)

    def page_mask(p):
        rows = []
        for g in range(NSA_KV):
            col = lambda j: sel[g:g + 1, p * blocks_per_page + j:p * blocks_per_page + j + 1]
            m = col(blocks_per_page - 1)
            for j in reversed(range(blocks_per_page - 1)):
                m = jnp.where(page_tok < (j + 1) * L_SEL, col(j), m)
            rows += [jnp.broadcast_to(m, (1, page))] * NSA_GROUP
        return jnp.concatenate(rows, axis=0) > 0.5

    new = new_ref[0]
    s_page = [jnp.where(page_mask(p), _dot(q8, buf[slot, p, 2 * NSA_KVW:3 * NSA_KVW, :].astype(BF16)), NEG_INF)
              for p in range(n_pages)]
    s_new = _row_dot(q8, new[:, 2 * NSA_KVW:3 * NSA_KVW])
    m = s_new
    for sp in s_page:
        m = jnp.maximum(m, jnp.max(sp, axis=1, keepdims=True))
    p_new = jnp.exp(s_new - m)
    den = p_new
    acc = p_new.astype(BF16).astype(F32) * new[:, 3 * NSA_KVW:4 * NSA_KVW].astype(BF16).astype(F32)
    for p, sp in enumerate(s_page):
        pt = jnp.exp(sp - m)
        den = den + jnp.sum(pt, axis=1, keepdims=True)
        acc = acc + _nt(pt.astype(BF16), buf[slot, p, 3 * NSA_KVW:4 * NSA_KVW, :].astype(BF16))
    o_sel = acc / den

    st = state_ref[0]
    wnew = wnew_ref[0]
    s = _dot(q8, st[0:LANES, :].astype(BF16))
    wpos = past - w_buf + lax.broadcasted_iota(jnp.int32, (1, w_buf), 1)
    mw = (wpos > past - WINDOW) & (wpos >= 0)
    s = jnp.where(mw, s, NEG_INF)
    s_new = _row_dot(q8, wnew[:, 0:LANES])
    m = jnp.maximum(s_new, jnp.max(s, axis=1, keepdims=True))
    p_new = jnp.exp(s_new - m)
    pw = jnp.exp(s - m)
    den = p_new + jnp.sum(pw, axis=1, keepdims=True)
    o_w = (_nt(pw.astype(BF16), st[LANES:2 * LANES, :].astype(BF16))
           + p_new.astype(BF16).astype(F32) * wnew[:, LANES:2 * LANES].astype(BF16).astype(F32)) / den

    gate = jnp.broadcast_to(_sigmoid(gate_ref[0]), (NSA_HEADS, LANES))

    def gcol(br):
        return jnp.sum(jnp.where(lane8 == 3 * head_row + br, gate, 0.0), axis=1, keepdims=True)

    o = gcol(0) * o_cmp + gcol(1) * o_sel + gcol(2) * o_w
    placed = []
    for hd in range(NSA_HEADS):
        r = o[hd:hd + 1]
        if hd % 2 != hd // NSA_GROUP:
            r = pltpu.roll(r, HEAD_DIM, 1)
        placed.append(r)
    pairs = [jnp.where(lane1 < HEAD_DIM, placed[2 * p], placed[2 * p + 1]) for p in range(NSA_HEADS // 2)]
    o_ref[0] = jnp.concatenate(pairs, axis=1).astype(BF16)

    nf = st.shape[0]
    on_diag = lax.broadcasted_iota(jnp.int32, (nf, nf), 0) == lax.broadcasted_iota(jnp.int32, (nf, nf), 1)
    new_col = jnp.sum(jnp.where(on_diag, jnp.broadcast_to(wnew, (nf, nf)), 0.0), axis=1, keepdims=True)
    tok = lax.broadcasted_iota(jnp.int32, (1, w_buf), 1)
    win_ref[0] = jnp.where(tok == w_buf - 1, new_col, pltpu.roll(st, w_buf - 1, 1))


def _nsa_decode(page_table, q, gate, new, wnew, state_feat, cache_feat, wlo, whi, pelo, pehi, c2s, past):
    db = q.shape[0]
    n_pages = page_table.shape[1]
    feat, page = cache_feat.shape[1:]
    w_buf = state_feat.shape[2]
    assert page % L_SEL == 0 and page % LANES == 0
    seq3 = lambda i, pt: (i, 0, 0)
    c3 = lambda i, pt: (0, 0, 0)
    c2 = lambda i, pt: (0, 0)
    once = lambda a, imap: pl.BlockSpec(a.shape, imap, pipeline_mode=pl.Buffered(1))
    grid_spec = pltpu.PrefetchScalarGridSpec(
        num_scalar_prefetch=1,
        grid=(db,),
        in_specs=[pl.BlockSpec((1, 1, NSA_QW), seq3), pl.BlockSpec((1, 1, LANES), seq3),
                  pl.BlockSpec((1, 1, 4 * NSA_KVW), seq3), pl.BlockSpec((1, 1, 2 * NSA_KVW), seq3),
                  pl.BlockSpec((1, 2 * NSA_KVW, w_buf), seq3), pl.BlockSpec(memory_space=pl.ANY),
                  once(wlo, c3), once(whi, c3), once(pelo, c3), once(pehi, c3), once(c2s, c2)],
        out_specs=[pl.BlockSpec((1, 1, NSA_QW), seq3), pl.BlockSpec((1, 2 * NSA_KVW, w_buf), seq3)],
        scratch_shapes=[pltpu.VMEM((2, n_pages, feat, page), F32), pltpu.SemaphoreType.DMA((2,)),
                        pltpu.VMEM((2 * NSA_KVW // LANES, n_pages * page, LANES), F32)],
    )
    return pl.pallas_call(
        functools.partial(_nsa_dec_kernel, n_pages=n_pages, past=past),
        grid_spec=grid_spec,
        out_shape=[jax.ShapeDtypeStruct((db, 1, NSA_QW), BF16), jax.ShapeDtypeStruct((db, 2 * NSA_KVW, w_buf), F32)],
        compiler_params=_cparams(("arbitrary",)),
        name="nsa_decode",
    )(page_table, q, gate, new, wnew, state_feat, cache_feat, wlo, whi, pelo, pehi, c2s)


SB_DEC_PAGES = 8


def _suffix_sums(x):
    r, n = x.shape
    lane = lax.broadcasted_iota(jnp.int32, (r, LANES), 1)
    run = jnp.zeros((r, 1), F32)
    outs = [None] * (n // LANES)
    for j in reversed(range(n // LANES)):
        xb = x[:, j * LANES:(j + 1) * LANES]
        inc = xb
        d = 1
        while d < LANES:
            inc = inc + jnp.where(lane + d < LANES, pltpu.roll(inc, LANES - d, 1), 0.0)
            d *= 2
        outs[j] = inc - xb + run
        run = run + inc[:, 0:1]
    return jnp.concatenate(outs, axis=1), run


def _sb_dec_kernel(pt_ref, q_ref, cache_ref, o_ref, buf, sem, acc_ref, carry_ref, *, pages):
    s_id = pl.program_id(0)
    c_id = pl.program_id(1)
    n_seq = pl.num_programs(0)
    n_chunk = pl.num_programs(1)
    step = s_id * n_chunk + c_id
    slot = step % 2

    def page_copy(seq, ch, p, sl):
        src = cache_ref.at[pt_ref[seq, (n_chunk - 1 - ch) * pages + p]]
        return pltpu.make_async_copy(src, buf.at[sl, p], sem.at[sl])

    def start_all(seq, ch, sl):
        for p in range(pages):
            page_copy(seq, ch, p, sl).start()

    @pl.when(step == 0)
    def _():
        start_all(0, 0, 0)

    nxt = step + 1

    @pl.when(nxt < n_seq * n_chunk)
    def _():
        start_all(nxt // n_chunk, nxt % n_chunk, 1 - slot)

    for p in range(pages):
        page_copy(s_id, c_id, p, slot).wait()

    @pl.when(c_id == 0)
    def _():
        acc_ref[...] = jnp.zeros_like(acc_ref)
        carry_ref[...] = jnp.zeros_like(carry_ref)

    head_row = lax.broadcasted_iota(jnp.int32, (SB_HEADS, SB_W), 0)
    lane = lax.broadcasted_iota(jnp.int32, (SB_HEADS, SB_W), 1)
    own = (lane // HEAD_DIM) == head_row
    q = jnp.broadcast_to(q_ref[0].astype(F32), (SB_HEADS, SB_W))
    q8 = jnp.where(own, q, 0.0).astype(BF16)

    z = jnp.concatenate([_dot(q8, buf[slot, p, 0:SB_W, :].astype(BF16)) for p in range(pages)], axis=1) * LOG2E
    sp = _softplus2(z)
    lm = -sp
    later, total = _suffix_sums(lm)
    a = jnp.exp2(z - sp + later + carry_ref[...]).astype(BF16)
    page = buf.shape[3]
    acc = acc_ref[...]
    for p in range(pages):
        acc = acc + _nt(a[:, p * page:(p + 1) * page], buf[slot, p, SB_W:2 * SB_W, :].astype(BF16))
    acc_ref[...] = acc
    carry_ref[...] += total

    @pl.when(c_id == n_chunk - 1)
    def _():
        o_ref[0] = jnp.sum(jnp.where(own, acc_ref[...], 0.0), axis=0, keepdims=True).astype(BF16)


def _sb_decode(page_table, q, cache_feat):
    db = q.shape[0]
    n_pages = page_table.shape[1]
    page = cache_feat.shape[2]
    pages = min(SB_DEC_PAGES, n_pages)
    assert n_pages % pages == 0 and page % LANES == 0
    seq3 = lambda i, c, pt: (i, 0, 0)
    grid_spec = pltpu.PrefetchScalarGridSpec(
        num_scalar_prefetch=1,
        grid=(db, n_pages // pages),
        in_specs=[pl.BlockSpec((1, 1, SB_W), seq3), pl.BlockSpec(memory_space=pl.ANY)],
        out_specs=pl.BlockSpec((1, 1, SB_W), seq3),
        scratch_shapes=[pltpu.VMEM((2, pages, 2 * SB_W, page), F32), pltpu.SemaphoreType.DMA((2,)),
                        pltpu.VMEM((SB_HEADS, SB_W), F32), pltpu.VMEM((SB_HEADS, 1), F32)],
    )
    return pl.pallas_call(
        functools.partial(_sb_dec_kernel, pages=pages),
        grid_spec=grid_spec,
        out_shape=jax.ShapeDtypeStruct((db, 1, SB_W), BF16),
        compiler_params=_cparams(("arbitrary", "arbitrary")),
        name="sb_decode",
    )(page_table, q, cache_feat)


def _rope_tables(pos):
    half = HEAD_DIM // 2
    inv = jnp.power(ROPE_THETA, -jnp.arange(half, dtype=F32) * (2.0 / HEAD_DIM))
    ang = pos.astype(F32)[:, None] * inv[None, :]
    cos, sin = jnp.cos(ang), jnp.sin(ang)
    reps = LANES // HEAD_DIM
    return jnp.tile(cos, (1, 2 * reps)), jnp.tile(jnp.concatenate([-sin, sin], axis=1), (1, reps))


def _cmp_to_sel(n_cmp, n_sel, rows, cols):
    s = np.arange(n_cmp)[:, None] * D_CMP
    b = np.arange(n_sel)[None, :] * L_SEL
    m = np.zeros((rows, cols), np.float32)
    m[:n_cmp, :n_sel] = (s < b + L_SEL) & (s + L_CMP > b)
    return jnp.asarray(m)


def _feature_major(a):
    n, tok = a.shape[:2]
    return jnp.transpose(a, (0, 2, 3, 4, 1)).reshape(n, -1, tok)


def _round_up(a, b):
    return -(-a // b) * b


def _row_tile(n):
    for tm in (512, 256, 128, 64, 32, 16, 8):
        if n % tm == 0:
            return tm
    raise ValueError(f"row count {n} is not a multiple of 8")


def kernel(x_prompt, x_sample, p_prompt, p_sample, cache_nsa, cache_sb, state_win, page_table, w_in, w_cmp, pe_cmp,
           w_oa, w_ob, w_out, g_pre_mix, g_post_mix, g_pre_ffn, g_post_ffn, w_up, w_down, w_ple, w_ple_gate):
    B, T, D = x_prompt.shape
    DB, TS = x_sample.shape[:2]
    assert TS == 1, "the decode kernels take one new token per sequence"
    depth = w_in.shape[0]
    n_phys, page = cache_nsa.shape[1:3]
    n_pages = page_table.shape[1]
    past = n_pages * page
    w_buf = state_win.shape[2]
    assert T % TOK_PER_ROW == 0 and page % TOK_PER_ROW == 0 and w_buf >= 2
    n2_p = T // TOK_PER_ROW
    n2_s = past // TOK_PER_ROW
    feat = 4 * NSA_KVW

    cos_p, sin_p = _rope_tables(jnp.arange(T, dtype=jnp.int32))
    cos_s, sin_s = _rope_tables(jnp.full((DB,), past, jnp.int32))
    wsel_p = _round_up(T // L_SEL, LANES)
    c2s_p = _cmp_to_sel(n2_p - 1, T // L_SEL, n2_p, wsel_p)
    n_sel_s = past // L_SEL + 1
    wsel_s = _round_up(n_sel_s, LANES)
    c2s_s = _cmp_to_sel(n2_s - 1, n_sel_s, n2_s, wsel_s)
    et = jnp.asarray(np.where(np.arange(T)[:, None] // L_SEL == np.arange(wsel_p)[None, :], NEG_INF, 0.0), BF16)
    tri = jnp.asarray(np.arange(SB_T)[:, None] > np.arange(SB_T)[None, :], BF16)

    tm_p = _row_tile(B * T)
    tm_s = _row_tile(DB)
    xp = x_prompt.reshape(B * T, D)
    xs = x_sample.reshape(DB, D)
    o = np.cumsum((0, NSA_QW, 6 * NSA_KVW, GATE_W, 3 * SB_W, 2 * D))
    nsa_p, sb_p, win_p, nsa_s, sb_s, win_s = [], [], [], [], [], []
    for i in range(depth):
        w = w_in[i]
        w_proj = jnp.concatenate([w[:, o[0]:o[2]], w[:, o[3]:o[4]],
                                  jnp.pad(w[:, o[2]:o[3]], ((0, 0), (0, LANES - GATE_W)))], axis=1).astype(BF16)
        w_mg = w[:, o[4]:o[5]].astype(BF16)
        wc = jnp.repeat(w_cmp[i], NSA_KV, axis=0)
        wblk = jnp.einsum('ab,alde->ladbe', jnp.eye(2 * NSA_KV, dtype=F32), wc).reshape(L_CMP, 2 * NSA_KVW, 2 * NSA_KVW).astype(BF16)
        wlo, whi = wblk[:TOK_PER_ROW], wblk[TOK_PER_ROW:]
        pe = jnp.broadcast_to(pe_cmp[i][:, :, None, :], (L_CMP, 2, NSA_KV, HEAD_DIM)).reshape(L_CMP, 1, 2 * NSA_KVW)
        pelo, pehi = pe[:TOK_PER_ROW], pe[TOK_PER_ROW:]
        g_pre = g_pre_mix[i].reshape(1, D)
        tail_w = (g_pre, w_mg, w_oa[i].astype(BF16), w_ob[i].astype(BF16), w_out[i].astype(BF16), g_post_mix[i].reshape(1, D))
        ffn_w = (g_pre_ffn[i].reshape(1, D), w_up[i].astype(BF16), w_down[i].astype(BF16), g_post_ffn[i].reshape(1, D),
                 w_ple[i].astype(BF16), w_ple_gate[i].astype(BF16))

        (q, nsa, win, gate, sbq, sbkv, ksel, vsel, kwin, vwin, sbk, sbv) = _project(xp, g_pre, w_proj, cos_p, sin_p, tm_p)
        b3 = lambda a: a.reshape(B, T, a.shape[-1])
        kc = _compress(nsa.reshape(B, n2_p, TOK_PER_ROW * feat), wlo, whi, pelo, pehi, min(128, n2_p))
        o_a = _nsa_prompt(b3(q), b3(gate), kc, b3(ksel), b3(vsel), b3(kwin), b3(vwin), c2s_p, et)
        o_b = _sb_prompt(b3(sbq), b3(sbk), b3(sbv), tri)
        xp = _mix(xp, o_a.reshape(B * T, NSA_QW), o_b.reshape(B * T, SB_W), *tail_w, tm_p)
        xp = _ffn(xp, p_prompt[i].reshape(B * T, -1), *ffn_w, tm_p)
        nsa_p.append(nsa.reshape(B, T, 4, NSA_KV, HEAD_DIM))
        sb_p.append(sbkv.reshape(B, T, 2, SB_HEADS, HEAD_DIM))
        win_p.append(b3(win)[:, T - min(WINDOW, T):].reshape(B, -1, 2, NSA_KV, HEAD_DIM))

        (q, nsa, win, gate, sbq, sbkv, _, _, _, _, _, _) = _project(xs, g_pre, w_proj, cos_s, sin_s, tm_s)
        r3 = lambda a: a.reshape(DB, 1, a.shape[-1])
        o_a, win_new = _nsa_decode(page_table, r3(q), r3(gate), r3(nsa), r3(win),
                                   _feature_major(state_win[i]), _feature_major(cache_nsa[i]),
                                   wlo, whi, pelo, pehi, c2s_s, past)
        o_b = _sb_decode(page_table, r3(sbq), _feature_major(cache_sb[i]))
        xs = _mix(xs, o_a.reshape(DB, NSA_QW), o_b.reshape(DB, SB_W), *tail_w, tm_s)
        xs = _ffn(xs, p_sample[i].reshape(DB, -1), *ffn_w, tm_s)
        nsa_s.append(nsa.reshape(DB, TS, 4, NSA_KV, HEAD_DIM))
        sb_s.append(sbkv.reshape(DB, TS, 2, SB_HEADS, HEAD_DIM))
        win_s.append(jnp.transpose(win_new.reshape(DB, 2, NSA_KV, HEAD_DIM, w_buf), (0, 4, 1, 2, 3)))
    return (xp.reshape(B, T, D), xs.reshape(DB, TS, D), jnp.stack(nsa_p), jnp.stack(sb_p), jnp.stack(win_p),
            jnp.stack(nsa_s), jnp.stack(sb_s), jnp.stack(win_s))
```

```python
import functools

import numpy as np
import jax
import jax.numpy as jnp
from jax import lax
from jax.experimental import pallas as pl
from jax.experimental.pallas import tpu as pltpu

F32 = jnp.float32
BF16 = jnp.bfloat16

HEAD_DIM = 64
NSA_HEADS = 8
NSA_KV = 2
NSA_GROUP = NSA_HEADS // NSA_KV
SB_HEADS = 8
L_CMP = 32
D_CMP = 16
L_SEL = 64
N_SEL = 16
N_LOCAL = 2
WINDOW = 512
ROPE_THETA = 10000.0
EPS = 1e-6
FORCE_BONUS = 1e4
NEG_INF = -1e30
LOWEST = -3e38
SCALE = HEAD_DIM ** -0.5
LOG2E = 1.4426950408889634
NSA_QW = NSA_HEADS * HEAD_DIM
NSA_KVW = NSA_KV * HEAD_DIM
SB_W = SB_HEADS * HEAD_DIM
GATE_W = 3 * NSA_HEADS

LANES = 128
TOK_PER_ROW = 16
VMEM_LIMIT = 56 * 1024 * 1024


def _cparams(sem):
    return pltpu.CompilerParams(dimension_semantics=sem, vmem_limit_bytes=VMEM_LIMIT)


def _dot(a, b):
    return jnp.dot(a, b, preferred_element_type=F32)


def _nt(a, b):
    return lax.dot_general(a, b, (((1,), (1,)), ((), ())), preferred_element_type=F32)


def _row_dot(a, row):
    return jnp.sum(a.astype(F32) * row.astype(BF16).astype(F32), axis=1, keepdims=True)


def _rms(x, g):
    return x * lax.rsqrt(jnp.mean(x * x, axis=-1, keepdims=True) + EPS) * g


def _sigmoid(x):
    return 1.0 / (1.0 + jnp.exp(-x))


def _masked_softmax(s, mask):
    s = jnp.where(mask, s, NEG_INF)
    e = jnp.exp(s - jnp.max(s, axis=-1, keepdims=True))
    return e / jnp.sum(e, axis=-1, keepdims=True)


def _proj_kernel(x_ref, g_ref, w_ref, cos_ref, sin_ref,
                 q_ref, nsa_ref, win_ref, gate_ref, sbq_ref,
                 ksel_ref, vsel_ref, kwin_ref, vwin_ref, sbk_ref, sbv_ref, nsat_ref, wint_ref, sbkvt_ref):
    h = _rms(x_ref[...], g_ref[...]).astype(BF16)
    cos = cos_ref[...]
    sin = sin_ref[...]
    lane = lax.broadcasted_iota(jnp.int32, cos.shape, 1)
    first_half = (lane & (HEAD_DIM - 1)) < HEAD_DIM // 2

    def rope(z):
        other = jnp.where(first_half, pltpu.roll(z, LANES - HEAD_DIM // 2, 1), pltpu.roll(z, HEAD_DIM // 2, 1))
        return z * cos + other * sin

    def blk(z, c):
        return z[:, c * LANES:(c + 1) * LANES]

    zq = _dot(h, w_ref[:, 0:NSA_QW])
    q = jnp.concatenate([rope(blk(zq, c)) for c in range(NSA_QW // LANES)], axis=1)
    q_ref[...] = (q * SCALE).astype(BF16)

    o = NSA_QW
    zkv = _dot(h, w_ref[:, o:o + 6 * NSA_KVW])
    kc, vc, ks, vs, kw, vw = rope(blk(zkv, 0)), blk(zkv, 1), rope(blk(zkv, 2)), blk(zkv, 3), rope(blk(zkv, 4)), blk(zkv, 5)
    nsa_ref[...] = jnp.concatenate([kc, vc, ks, vs], axis=1)
    win_ref[...] = jnp.concatenate([kw, vw], axis=1)
    for c, z in enumerate((kc, vc, ks, vs)):
        nsat_ref[0, c * LANES:(c + 1) * LANES, :] = z.T
    for c, z in enumerate((kw, vw)):
        wint_ref[0, c * LANES:(c + 1) * LANES, :] = z.T
    ksel_ref[...] = ks.astype(BF16)
    vsel_ref[...] = vs.astype(BF16)
    kwin_ref[...] = kw.astype(BF16)
    vwin_ref[...] = vw.astype(BF16)

    o += 6 * NSA_KVW
    zsb = _dot(h, w_ref[:, o:o + 3 * SB_W])
    sbq_ref[...] = (zsb[:, 0:SB_W] * SCALE).astype(BF16)
    for c in range(2 * SB_W // LANES):
        sbkvt_ref[0, c * LANES:(c + 1) * LANES, :] = zsb[:, SB_W + c * LANES:SB_W + (c + 1) * LANES].T
    sbk_ref[...] = zsb[:, SB_W:2 * SB_W].astype(BF16)
    sbv_ref[...] = zsb[:, 2 * SB_W:3 * SB_W].astype(BF16)

    o += 3 * SB_W
    gate_ref[...] = _dot(h, w_ref[:, o:o + LANES])


def _project(x2d, g, w_proj, cos, sin, tm):
    n, d = x2d.shape
    n_tab = cos.shape[0] // tm
    row = lambda i: (i, 0)
    const = lambda i: (0, 0)
    tab = lambda i: (i % n_tab, 0)
    widths = [(NSA_QW, BF16), (4 * NSA_KVW, F32), (2 * NSA_KVW, F32), (LANES, F32), (SB_W, BF16),
              (NSA_KVW, BF16), (NSA_KVW, BF16), (NSA_KVW, BF16), (NSA_KVW, BF16), (SB_W, BF16), (SB_W, BF16)]
    t_feats = [4 * NSA_KVW, 2 * NSA_KVW, 2 * SB_W]
    grp = lambda i: (i // n_tab, 0, i % n_tab)
    return pl.pallas_call(
        _proj_kernel,
        grid=(n // tm,),
        in_specs=[pl.BlockSpec((tm, d), row), pl.BlockSpec((1, d), const),
                  pl.BlockSpec(w_proj.shape, const, pipeline_mode=pl.Buffered(1)),
                  pl.BlockSpec((tm, LANES), tab), pl.BlockSpec((tm, LANES), tab)],
        out_specs=[pl.BlockSpec((tm, wd), row) for wd, _ in widths] + [pl.BlockSpec((1, f, tm), grp) for f in t_feats],
        out_shape=[jax.ShapeDtypeStruct((n, wd), dt) for wd, dt in widths]
        + [jax.ShapeDtypeStruct((n // (n_tab * tm), f, n_tab * tm), F32) for f in t_feats],
        compiler_params=_cparams(("parallel",)),
        name="proj",
    )(x2d, g, w_proj, cos, sin)


def _compress_rows(x_of, wlo_ref, whi_ref, pelo_ref, pehi_ref, rows):
    ylo = jnp.zeros((rows, 2 * NSA_KVW), F32)
    yhi = jnp.zeros((rows, 2 * NSA_KVW), F32)
    for l in range(TOK_PER_ROW):
        xl = x_of(l)
        ylo = ylo + _dot((xl + pelo_ref[l]).astype(BF16), wlo_ref[l])
        yhi = yhi + _dot((xl + pehi_ref[l]).astype(BF16), whi_ref[l])
    return ylo, yhi


def _compress_kernel(x_ref, wlo_ref, whi_ref, pelo_ref, pehi_ref, out_ref, ylo_ref, yhi_ref):
    t = pl.program_id(1)
    rows = x_ref.shape[1]
    n2 = ylo_ref.shape[0]
    feat = 4 * NSA_KVW
    ylo, yhi = _compress_rows(lambda l: x_ref[0, :, l * feat:l * feat + 2 * NSA_KVW],
                              wlo_ref, whi_ref, pelo_ref, pehi_ref, rows)
    r0 = pl.multiple_of(t * rows, rows)
    ylo_ref[pl.ds(r0, rows), :] = ylo
    yhi_ref[pl.ds(r0, rows), :] = yhi

    @pl.when(t == pl.num_programs(1) - 1)
    def _():
        out_ref[0] = (ylo_ref[...] + pltpu.roll(yhi_ref[...], n2 - 1, 0)).astype(BF16)


def _compress(nsa_rows, wlo, whi, pelo, pehi, rows):
    b, n2, wd = nsa_rows.shape
    c3 = lambda i, t: (0, 0, 0)
    return pl.pallas_call(
        _compress_kernel,
        grid=(b, n2 // rows),
        in_specs=[pl.BlockSpec((1, rows, wd), lambda i, t: (i, t, 0)),
                  pl.BlockSpec(wlo.shape, c3), pl.BlockSpec(whi.shape, c3),
                  pl.BlockSpec(pelo.shape, c3), pl.BlockSpec(pehi.shape, c3)],
        out_specs=pl.BlockSpec((1, n2, 2 * NSA_KVW), lambda i, t: (i, 0, 0)),
        out_shape=jax.ShapeDtypeStruct((b, n2, 2 * NSA_KVW), BF16),
        scratch_shapes=[pltpu.VMEM((n2, 2 * NSA_KVW), F32), pltpu.VMEM((n2, 2 * NSA_KVW), F32)],
        compiler_params=_cparams(("parallel", "arbitrary")),
        name="compress",
    )(nsa_rows, wlo, whi, pelo, pehi)


def _select_blocks(imp, tq, n_sel):
    w = imp.shape[1]
    blk = lax.broadcasted_iota(jnp.int32, (1, w), 1)
    cur = tq // L_SEL
    forced = (blk == 0) | ((blk <= cur) & (blk > cur - N_LOCAL))
    valid = (blk * L_SEL <= tq) & (blk < n_sel)
    x = jnp.where(valid, jnp.where(forced, imp + FORCE_BONUS, imp), NEG_INF)
    x = jnp.where(blk < n_sel, x, LOWEST)
    sel = jnp.zeros(x.shape, jnp.bool_)
    for _ in range(min(N_SEL, n_sel)):
        m = jnp.max(x, axis=1, keepdims=True)
        first = jnp.min(jnp.where(x == m, blk, w), axis=1, keepdims=True)
        hit = blk == first
        sel = sel | hit
        x = jnp.where(hit, LOWEST, x)
    return sel & valid


def _select_blocks_t(imp_t, tq_row, n_sel):
    w = imp_t.shape[0]
    blk = lax.broadcasted_iota(jnp.int32, (w, 1), 0)
    cur = tq_row // L_SEL
    forced = (blk == 0) | ((blk <= cur) & (blk > cur - N_LOCAL))
    valid = (blk * L_SEL <= tq_row) & (blk < n_sel)
    x = jnp.where(valid, jnp.where(forced, imp_t + FORCE_BONUS, imp_t), NEG_INF)
    x = jnp.where(blk < n_sel, x, LOWEST)
    for _ in range(min(N_SEL, n_sel)):
        m = jnp.max(x, axis=0, keepdims=True)
        first = jnp.min(jnp.where(x == m, blk, w), axis=0, keepdims=True)
        x = jnp.where(blk == first, LOWEST, x)
    return (x == LOWEST) & valid


NSA_TQ = 128
NSA_KC = 1024


def _nsa_kernel(q_ref, gate_ref, kc_ref, ksel_ref, vsel_ref, kwin_ref, vwin_ref, c2s_ref, et_ref, o_ref, *, seq):
    i = pl.program_id(1)
    tq_n = NSA_TQ
    q0 = i * tq_n
    n2 = kc_ref.shape[1]
    n_cmp = n2 - 1
    n_sel = seq // L_SEL
    wsel = c2s_ref.shape[1]
    rows = NSA_HEADS * tq_n

    qf = q_ref[0].astype(F32)
    gate = _sigmoid(gate_ref[0])
    lane = lax.broadcasted_iota(jnp.int32, (tq_n, LANES), 1)
    tq = q0 + lax.broadcasted_iota(jnp.int32, (tq_n, 1), 0)
    t8 = jnp.concatenate([tq] * NSA_HEADS, axis=0)
    tq_row = q0 + lax.broadcasted_iota(jnp.int32, (1, tq_n), 1)
    blkid = lax.broadcasted_iota(jnp.int32, (1, wsel), 1)

    def per_head(by_group):
        return jnp.concatenate([by_group[hd // NSA_GROUP] for hd in range(NSA_HEADS)], axis=0)

    def online(carry, s, v):
        m, l, acc = carry
        mn = jnp.maximum(m, jnp.max(s, axis=1, keepdims=True))
        a = jnp.exp(m - mn)
        p = jnp.exp(s - mn)
        return mn, a * l + jnp.sum(p, axis=1, keepdims=True), a * acc + _dot(p.astype(BF16), v)

    embs = []
    for hd in range(NSA_HEADS):
        g = hd // NSA_GROUP
        b = qf[:, (hd // 2) * LANES:(hd // 2 + 1) * LANES]
        if hd % 2 != g:
            b = pltpu.roll(b, HEAD_DIM, 1)
        embs.append(jnp.where((lane // HEAD_DIM) == g, b, 0.0))
    q8 = jnp.concatenate(embs, axis=0).astype(BF16)

    s = _nt(q8, kc_ref[0, :, 0:LANES])
    ncol = lax.broadcasted_iota(jnp.int32, (1, n2), 1)
    mc = (ncol * D_CMP + (L_CMP - 1) <= t8) & (ncol < n_cmp)
    p = jnp.where(mc, _masked_softmax(s, mc), 0.0)
    o_cmp = _dot(p.astype(BF16), kc_ref[0, :, LANES:2 * LANES])
    psums = []
    for g in range(NSA_KV):
        blocks = [p[hd * tq_n:(hd + 1) * tq_n] for hd in range(g * NSA_GROUP, (g + 1) * NSA_GROUP)]
        psums.append(functools.reduce(lambda a, b: a + b, blocks))
    imp = jnp.dot(jnp.concatenate(psums, axis=0), c2s_ref[...], preferred_element_type=F32,
                  precision=lax.Precision.HIGHEST)

    sel_t = _select_blocks_t(imp.T, jnp.concatenate([tq_row] * NSA_KV, axis=1), n_sel)
    notsel = jnp.where(sel_t, 0.0, 1.0).T
    notsel_g = [notsel[g * tq_n:(g + 1) * tq_n] for g in range(NSA_KV)]
    before_diag = [jnp.where(blkid >= q0 // L_SEL, 1.0, ns) for ns in notsel_g]
    aug_main = jnp.concatenate([q8, per_head(before_diag).astype(BF16)], axis=1)
    aug_diag = jnp.concatenate([q8, per_head(notsel_g).astype(BF16)], axis=1)

    def main_body(c, carry):
        k0 = pl.multiple_of(c * NSA_KC, NSA_KC)
        rhs = jnp.concatenate([ksel_ref[0, pl.ds(k0, NSA_KC), :], et_ref[pl.ds(k0, NSA_KC), :]], axis=1)
        return online(carry, _nt(aug_main, rhs), vsel_ref[0, pl.ds(k0, NSA_KC), :])

    init = (jnp.full((rows, 1), LOWEST, F32), jnp.zeros((rows, 1), F32), jnp.zeros((rows, LANES), F32))
    carry = lax.fori_loop(0, (q0 + NSA_KC - 1) // NSA_KC, main_body, init)
    kd = pl.multiple_of(q0, tq_n)
    rhs = jnp.concatenate([ksel_ref[0, pl.ds(kd, tq_n), :], et_ref[pl.ds(kd, tq_n), :]], axis=1)
    s = jnp.where(tq_row <= t8, _nt(aug_diag, rhs), NEG_INF)
    _, l, acc = online(carry, s, vsel_ref[0, pl.ds(kd, tq_n), :])
    o_sel = acc / l

    wlen = WINDOW + tq_n
    w0 = pl.multiple_of(jnp.maximum(q0 - WINDOW, 0), tq_n)
    wpos = w0 + lax.broadcasted_iota(jnp.int32, (1, wlen), 1)
    mw = (wpos <= t8) & (wpos > t8 - WINDOW)
    s = _nt(q8, kwin_ref[0, pl.ds(w0, wlen), :])
    o_w = _dot(_masked_softmax(s, mw).astype(BF16), vwin_ref[0, pl.ds(w0, wlen), :])

    outs = []
    for hd in range(NSA_HEADS):
        r = slice(hd * tq_n, (hd + 1) * tq_n)
        o = (gate[:, 3 * hd:3 * hd + 1] * o_cmp[r] + gate[:, 3 * hd + 1:3 * hd + 2] * o_sel[r]
             + gate[:, 3 * hd + 2:3 * hd + 3] * o_w[r])
        if hd % 2 != hd // NSA_GROUP:
            o = pltpu.roll(o, HEAD_DIM, 1)
        outs.append(o)
    pairs = [jnp.where(lane < HEAD_DIM, outs[2 * p], outs[2 * p + 1]) for p in range(NSA_HEADS // 2)]
    o_ref[0] = jnp.concatenate(pairs, axis=1).astype(BF16)


def _nsa_prompt(q, gate, kc, ksel, vsel, kwin, vwin, c2s, et):
    b, t, _ = q.shape
    assert t % NSA_KC == 0 and t >= WINDOW + NSA_TQ
    blk = lambda i, j: (i, j, 0)
    per_b = lambda i, j: (i, 0, 0)
    c2 = lambda i, j: (0, 0)
    kv_spec = pl.BlockSpec((1, t, LANES), per_b)
    return pl.pallas_call(
        functools.partial(_nsa_kernel, seq=t),
        grid=(b, t // NSA_TQ),
        in_specs=[pl.BlockSpec((1, NSA_TQ, NSA_QW), blk), pl.BlockSpec((1, NSA_TQ, LANES), blk),
                  pl.BlockSpec((1,) + kc.shape[1:], per_b), kv_spec, kv_spec, kv_spec, kv_spec,
                  pl.BlockSpec(c2s.shape, c2), pl.BlockSpec(et.shape, c2)],
        out_specs=pl.BlockSpec((1, NSA_TQ, NSA_QW), blk),
        out_shape=jax.ShapeDtypeStruct((b, t, NSA_QW), BF16),
        compiler_params=_cparams(("parallel", "arbitrary")),
        name="nsa_prompt",
    )(q, gate, kc, ksel, vsel, kwin, vwin, c2s, et)


SB_T = 256


def _softplus2(z2):
    return jnp.maximum(z2, 0.0) + jnp.log2(1.0 + jnp.exp2(-jnp.abs(z2)))


def _sb_kernel(q_ref, k_ref, v_ref, u_ref, o_ref):
    i = pl.program_id(2)
    tb = SB_T
    q = q_ref[0].astype(F32)
    lane = lax.broadcasted_iota(jnp.int32, (tb, LANES), 1)
    u = u_ref[...]
    qpos = lax.broadcasted_iota(jnp.int32, (tb, 1), 0)
    kpos = lax.broadcasted_iota(jnp.int32, (1, tb), 1)
    below = jnp.concatenate([kpos < qpos] * 2, axis=0)
    q2 = jnp.concatenate([jnp.where((lane // HEAD_DIM) == hh, q, 0.0) for hh in range(2)], axis=0).astype(BF16)

    def scores(c):
        k0 = pl.multiple_of(c * tb, tb)
        return _nt(q2, k_ref[0, pl.ds(k0, tb), :]) * LOG2E

    def chunk(c, z, carry, acc, diagonal):
        sp = _softplus2(z)
        lm = -sp
        if diagonal:
            lm = jnp.where(below, lm, 0.0)
        after = _dot(lm.astype(BF16), u) + carry
        a = jnp.exp2(z - sp + after)
        if diagonal:
            a = jnp.where(below, a, 0.0)
        k0 = pl.multiple_of(c * tb, tb)
        acc = acc + _dot(a.astype(BF16), v_ref[0, pl.ds(k0, tb), :])
        return carry + jnp.sum(lm, axis=1, keepdims=True), acc

    def body(n, state):
        z, carry, acc = state
        c = i - 1 - n
        z_next = scores(jnp.maximum(c - 1, 0))
        carry, acc = chunk(c, z, carry, acc, False)
        return z_next, carry, acc

    carry, acc = chunk(i, scores(i), jnp.zeros((2 * tb, 1), F32), jnp.zeros((2 * tb, LANES), F32), True)
    _, _, acc = lax.fori_loop(0, i, body, (scores(jnp.maximum(i - 1, 0)), carry, acc))
    o_ref[0] = jnp.where(lane < HEAD_DIM, acc[0:tb], acc[tb:2 * tb]).astype(BF16)


def _sb_prompt(q, k, v, u):
    b, t, wd = q.shape
    assert t % SB_T == 0
    blk = lambda i, p, j: (i, j, p)
    per = lambda i, p, j: (i, 0, p)
    return pl.pallas_call(
        _sb_kernel,
        grid=(b, wd // LANES, t // SB_T),
        in_specs=[pl.BlockSpec((1, SB_T, LANES), blk), pl.BlockSpec((1, t, LANES), per),
                  pl.BlockSpec((1, t, LANES), per), pl.BlockSpec(u.shape, lambda i, p, j: (0, 0))],
        out_specs=pl.BlockSpec((1, SB_T, LANES), blk),
        out_shape=jax.ShapeDtypeStruct((b, t, wd), BF16),
        compiler_params=_cparams(("parallel", "parallel", "arbitrary")),
        name="sb_prompt",
    )(q, k, v, u)


def _mix_kernel(x_ref, oa_ref, ob_ref, gpre_ref, wmg_ref, woa_ref, wob_ref, wout_ref, gpost_ref, o_ref):
    x = x_ref[...]
    d = x.shape[1]
    h = _rms(x, gpre_ref[...]).astype(BF16)
    mg = _dot(h, wmg_ref[...])
    ya = _dot(oa_ref[...], woa_ref[...])
    yb = _dot(ob_ref[...], wob_ref[...])
    merged = _sigmoid(mg[:, 0:d]) * ya + _sigmoid(mg[:, d:2 * d]) * yb
    o_ref[...] = x + _rms(_dot(merged.astype(BF16), wout_ref[...]), gpost_ref[...])


def _mix(x2d, oa, ob, gpre, wmg, woa, wob, wout, gpost, tm):
    n, d = x2d.shape
    row = lambda i: (i, 0)
    const = lambda i: (0, 0)
    wspec = lambda w: pl.BlockSpec(w.shape, const, pipeline_mode=pl.Buffered(1))
    return pl.pallas_call(
        _mix_kernel,
        grid=(n // tm,),
        in_specs=[pl.BlockSpec((tm, d), row), pl.BlockSpec((tm, oa.shape[1]), row), pl.BlockSpec((tm, ob.shape[1]), row),
                  pl.BlockSpec((1, d), const), wspec(wmg), wspec(woa), wspec(wob), wspec(wout), pl.BlockSpec((1, d), const)],
        out_specs=pl.BlockSpec((tm, d), row),
        out_shape=jax.ShapeDtypeStruct((n, d), F32),
        compiler_params=_cparams(("parallel",)),
        name="mix",
    )(x2d, oa, ob, gpre, wmg, woa, wob, wout, gpost)


FFN_CHUNK = 1024


def _ffn_kernel(x_ref, p_ref, gpre_ref, wup_ref, wdown_ref, gpost_ref, wple_ref, wpg_ref, o_ref):
    x = x_ref[...]
    h = _rms(x, gpre_ref[...]).astype(BF16)
    f = jnp.zeros(x.shape, F32)
    for c in range(wup_ref.shape[1] // FFN_CHUNK):
        cs = slice(c * FFN_CHUNK, (c + 1) * FFN_CHUNK)
        up = jnp.maximum(_dot(h, wup_ref[:, cs]), 0.0)
        f = f + _dot((up * up).astype(BF16), wdown_ref[cs, :])
    x = x + _rms(f, gpost_ref[...])
    ple = _dot(p_ref[...].astype(BF16), wple_ref[...])
    o_ref[...] = x + ple * _sigmoid(_dot(x.astype(BF16), wpg_ref[...]))


def _ffn(x2d, p2d, gpre, wup, wdown, gpost, wple, wpg, tm):
    n, d = x2d.shape
    row = lambda i: (i, 0)
    const = lambda i: (0, 0)
    wspec = lambda w: pl.BlockSpec(w.shape, const, pipeline_mode=pl.Buffered(1))
    return pl.pallas_call(
        _ffn_kernel,
        grid=(n // tm,),
        in_specs=[pl.BlockSpec((tm, d), row), pl.BlockSpec((tm, p2d.shape[1]), row), pl.BlockSpec((1, d), const),
                  wspec(wup), wspec(wdown), pl.BlockSpec((1, d), const), wspec(wple), wspec(wpg)],
        out_specs=pl.BlockSpec((tm, d), row),
        out_shape=jax.ShapeDtypeStruct((n, d), F32),
        compiler_params=_cparams(("parallel",)),
        name="ffn",
    )(x2d, p2d, gpre, wup, wdown, gpost, wple, wpg)


def _nsa_dec_kernel(pt_ref, q_ref, gate_ref, new_ref, wnew_ref, state_ref, cache_ref,
                    wlo_ref, whi_ref, pelo_ref, pehi_ref, c2s_ref,
                    o_ref, win_ref, buf, sem, tok_ref, *, n_pages, past):
    s_id = pl.program_id(0)
    n_seq = pl.num_programs(0)
    slot = s_id % 2
    page = buf.shape[3]
    n2 = tok_ref.shape[1] // TOK_PER_ROW
    n_cmp = n2 - 1
    n_sel = past // L_SEL + 1
    w_buf = state_ref.shape[2]
    cmp_w = 2 * NSA_KVW

    def page_copy(seq, p, sl):
        return pltpu.make_async_copy(cache_ref.at[pt_ref[seq, p]], buf.at[sl, p], sem.at[sl])

    def start_all(seq, sl):
        lax.fori_loop(0, n_pages, lambda p, c: (page_copy(seq, p, sl).start(), c)[1], 0)

    @pl.when(s_id == 0)
    def _():
        start_all(0, 0)

    @pl.when(s_id + 1 < n_seq)
    def _():
        start_all(s_id + 1, 1 - slot)

    lax.fori_loop(0, n_pages, lambda p, c: (page_copy(s_id, p, slot).wait(), c)[1], 0)

    halves = range(cmp_w // LANES)

    def to_token_major(p, c):
        r0 = pl.multiple_of(p * page, page)
        for h in halves:
            tok_ref[h, pl.ds(r0, page), :] = buf[slot, p, h * LANES:(h + 1) * LANES, :].T
        return c

    lax.fori_loop(0, n_pages, to_token_major, 0)
    ylo, yhi = _compress_rows(
        lambda l: jnp.concatenate([tok_ref[h, pl.ds(l, n2, stride=TOK_PER_ROW), :] for h in halves], axis=1),
        wlo_ref, whi_ref, pelo_ref, pehi_ref, n2)
    kc = (ylo + pltpu.roll(yhi, n2 - 1, 0)).astype(BF16)

    qf = q_ref[0].astype(F32)
    lane1 = lax.broadcasted_iota(jnp.int32, (1, LANES), 1)
    rows = []
    for hd in range(NSA_HEADS):
        g = hd // NSA_GROUP
        b = qf[:, (hd // 2) * LANES:(hd // 2 + 1) * LANES]
        if hd % 2 != g:
            b = pltpu.roll(b, HEAD_DIM, 1)
        rows.append(jnp.where((lane1 // HEAD_DIM) == g, b, 0.0))
    q8 = jnp.concatenate(rows, axis=0).astype(BF16)
    head_row = lax.broadcasted_iota(jnp.int32, (NSA_HEADS, 1), 0)
    tq = jnp.full((NSA_KV, 1), past, jnp.int32)

    s = _nt(q8, kc[:, 0:LANES])
    ncol = lax.broadcasted_iota(jnp.int32, (1, n2), 1)
    mc = (ncol * D_CMP + (L_CMP - 1) <= past) & (ncol < n_cmp)
    p = jnp.where(mc, _masked_softmax(s, mc), 0.0)
    o_cmp = _dot(p.astype(BF16), kc[:, LANES:2 * LANES])
    psum = jnp.concatenate([jnp.sum(p[g * NSA_GROUP:(g + 1) * NSA_GROUP], axis=0, keepdims=True)
                            for g in range(NSA_KV)], axis=0)
    imp = jnp.dot(psum, c2s_ref[...], preferred_element_type=F32, precision=lax.Precision.HIGHEST)

    sel = jnp.where(_select_blocks(imp, tq, n_sel), 1.0, 0.0)
    lane8 = lax.broadcasted_iota(jnp.int32, (NSA_HEADS, LANES), 1)
    blocks_per_page = page // L_SEL
    page_tok = lax.broadcasted_iota(jnp.int32, (1, page), 1)

    def page_mask(p):
        rows = []
        for g in range(NSA_KV):
            col = lambda j: sel[g:g + 1, p * blocks_per_page + j:p * blocks_per_page + j + 1]
            m = col(blocks_per_page - 1)
            for j in reversed(range(blocks_per_page - 1)):
                m = jnp.where(page_tok < (j + 1) * L_SEL, col(j), m)
            rows += [jnp.broadcast_to(m, (1, page))] * NSA_GROUP
        return jnp.concatenate(rows, axis=0) > 0.5

    new = new_ref[0]
    s_page = [jnp.where(page_mask(p), _dot(q8, buf[slot, p, 2 * NSA_KVW:3 * NSA_KVW, :].astype(BF16)), NEG_INF)
              for p in range(n_pages)]
    s_new = _row_dot(q8, new[:, 2 * NSA_KVW:3 * NSA_KVW])
    m = s_new
    for sp in s_page:
        m = jnp.maximum(m, jnp.max(sp, axis=1, keepdims=True))
    p_new = jnp.exp(s_new - m)
    den = p_new
    acc = p_new.astype(BF16).astype(F32) * new[:, 3 * NSA_KVW:4 * NSA_KVW].astype(BF16).astype(F32)
    for p, sp in enumerate(s_page):
        pt = jnp.exp(sp - m)
        den = den + jnp.sum(pt, axis=1, keepdims=True)
        acc = acc + _nt(pt.astype(BF16), buf[slot, p, 3 * NSA_KVW:4 * NSA_KVW, :].astype(BF16))
    o_sel = acc / den

    st = state_ref[0]
    wnew = wnew_ref[0]
    s = _dot(q8, st[0:LANES, :].astype(BF16))
    wpos = past - w_buf + lax.broadcasted_iota(jnp.int32, (1, w_buf), 1)
    mw = (wpos > past - WINDOW) & (wpos >= 0)
    s = jnp.where(mw, s, NEG_INF)
    s_new = _row_dot(q8, wnew[:, 0:LANES])
    m = jnp.maximum(s_new, jnp.max(s, axis=1, keepdims=True))
    p_new = jnp.exp(s_new - m)
    pw = jnp.exp(s - m)
    den = p_new + jnp.sum(pw, axis=1, keepdims=True)
    o_w = (_nt(pw.astype(BF16), st[LANES:2 * LANES, :].astype(BF16))
           + p_new.astype(BF16).astype(F32) * wnew[:, LANES:2 * LANES].astype(BF16).astype(F32)) / den

    gate = jnp.broadcast_to(_sigmoid(gate_ref[0]), (NSA_HEADS, LANES))

    def gcol(br):
        return jnp.sum(jnp.where(lane8 == 3 * head_row + br, gate, 0.0), axis=1, keepdims=True)

    o = gcol(0) * o_cmp + gcol(1) * o_sel + gcol(2) * o_w
    placed = []
    for hd in range(NSA_HEADS):
        r = o[hd:hd + 1]
        if hd % 2 != hd // NSA_GROUP:
            r = pltpu.roll(r, HEAD_DIM, 1)
        placed.append(r)
    pairs = [jnp.where(lane1 < HEAD_DIM, placed[2 * p], placed[2 * p + 1]) for p in range(NSA_HEADS // 2)]
    o_ref[0] = jnp.concatenate(pairs, axis=1).astype(BF16)

    nf = st.shape[0]
    on_diag = lax.broadcasted_iota(jnp.int32, (nf, nf), 0) == lax.broadcasted_iota(jnp.int32, (nf, nf), 1)
    new_col = jnp.sum(jnp.where(on_diag, jnp.broadcast_to(wnew, (nf, nf)), 0.0), axis=1, keepdims=True)
    tok = lax.broadcasted_iota(jnp.int32, (1, w_buf), 1)
    win_ref[0] = jnp.where(tok == w_buf - 1, new_col, pltpu.roll(st, w_buf - 1, 1))


def _nsa_decode(page_table, q, gate, new, wnew, state_feat, cache_feat, wlo, whi, pelo, pehi, c2s, past):
    db = q.shape[0]
    n_pages = page_table.shape[1]
    feat, page = cache_feat.shape[1:]
    w_buf = state_feat.shape[2]
    assert page % L_SEL == 0 and page % LANES == 0
    seq3 = lambda i, pt: (i, 0, 0)
    c3 = lambda i, pt: (0, 0, 0)
    c2 = lambda i, pt: (0, 0)
    once = lambda a, imap: pl.BlockSpec(a.shape, imap, pipeline_mode=pl.Buffered(1))
    grid_spec = pltpu.PrefetchScalarGridSpec(
        num_scalar_prefetch=1,
        grid=(db,),
        in_specs=[pl.BlockSpec((1, 1, NSA_QW), seq3), pl.BlockSpec((1, 1, LANES), seq3),
                  pl.BlockSpec((1, 1, 4 * NSA_KVW), seq3), pl.BlockSpec((1, 1, 2 * NSA_KVW), seq3),
                  pl.BlockSpec((1, 2 * NSA_KVW, w_buf), seq3), pl.BlockSpec(memory_space=pl.ANY),
                  once(wlo, c3), once(whi, c3), once(pelo, c3), once(pehi, c3), once(c2s, c2)],
        out_specs=[pl.BlockSpec((1, 1, NSA_QW), seq3), pl.BlockSpec((1, 2 * NSA_KVW, w_buf), seq3)],
        scratch_shapes=[pltpu.VMEM((2, n_pages, feat, page), F32), pltpu.SemaphoreType.DMA((2,)),
                        pltpu.VMEM((2 * NSA_KVW // LANES, n_pages * page, LANES), F32)],
    )
    return pl.pallas_call(
        functools.partial(_nsa_dec_kernel, n_pages=n_pages, past=past),
        grid_spec=grid_spec,
        out_shape=[jax.ShapeDtypeStruct((db, 1, NSA_QW), BF16), jax.ShapeDtypeStruct((db, 2 * NSA_KVW, w_buf), F32)],
        compiler_params=_cparams(("arbitrary",)),
        name="nsa_decode",
    )(page_table, q, gate, new, wnew, state_feat, cache_feat, wlo, whi, pelo, pehi, c2s)


SB_DEC_PAGES = 16


def _suffix_sums(x):
    r, n = x.shape
    lane = lax.broadcasted_iota(jnp.int32, (r, LANES), 1)
    run = jnp.zeros((r, 1), F32)
    outs = [None] * (n // LANES)
    for j in reversed(range(n // LANES)):
        xb = x[:, j * LANES:(j + 1) * LANES]
        inc = xb
        d = 1
        while d < LANES:
            inc = inc + jnp.where(lane + d < LANES, pltpu.roll(inc, LANES - d, 1), 0.0)
            d *= 2
        outs[j] = inc - xb + run
        run = run + inc[:, 0:1]
    return jnp.concatenate(outs, axis=1), run


def _sb_dec_kernel(pt_ref, q_ref, cache_ref, o_ref, buf, sem, acc_ref, carry_ref, *, pages):
    s_id = pl.program_id(0)
    c_id = pl.program_id(1)
    n_seq = pl.num_programs(0)
    n_chunk = pl.num_programs(1)
    step = s_id * n_chunk + c_id
    slot = step % 2

    def page_copy(seq, ch, p, sl):
        src = cache_ref.at[pt_ref[seq, (n_chunk - 1 - ch) * pages + p]]
        return pltpu.make_async_copy(src, buf.at[sl, p], sem.at[sl])

    def start_all(seq, ch, sl):
        for p in range(pages):
            page_copy(seq, ch, p, sl).start()

    @pl.when(step == 0)
    def _():
        start_all(0, 0, 0)

    nxt = step + 1

    @pl.when(nxt < n_seq * n_chunk)
    def _():
        start_all(nxt // n_chunk, nxt % n_chunk, 1 - slot)

    for p in range(pages):
        page_copy(s_id, c_id, p, slot).wait()

    @pl.when(c_id == 0)
    def _():
        acc_ref[...] = jnp.zeros_like(acc_ref)
        carry_ref[...] = jnp.zeros_like(carry_ref)

    head_row = lax.broadcasted_iota(jnp.int32, (SB_HEADS, SB_W), 0)
    lane = lax.broadcasted_iota(jnp.int32, (SB_HEADS, SB_W), 1)
    own = (lane // HEAD_DIM) == head_row
    q = jnp.broadcast_to(q_ref[0].astype(F32), (SB_HEADS, SB_W))
    q8 = jnp.where(own, q, 0.0).astype(BF16)

    z = jnp.concatenate([_dot(q8, buf[slot, p, 0:SB_W, :].astype(BF16)) for p in range(pages)], axis=1) * LOG2E
    sp = _softplus2(z)
    lm = -sp
    later, total = _suffix_sums(lm)
    a = jnp.exp2(z - sp + later + carry_ref[...]).astype(BF16)
    page = buf.shape[3]
    acc = acc_ref[...]
    for p in range(pages):
        acc = acc + _nt(a[:, p * page:(p + 1) * page], buf[slot, p, SB_W:2 * SB_W, :].astype(BF16))
    acc_ref[...] = acc
    carry_ref[...] += total

    @pl.when(c_id == n_chunk - 1)
    def _():
        o_ref[0] = jnp.sum(jnp.where(own, acc_ref[...], 0.0), axis=0, keepdims=True).astype(BF16)


def _sb_decode(page_table, q, cache_feat):
    db = q.shape[0]
    n_pages = page_table.shape[1]
    page = cache_feat.shape[2]
    pages = min(SB_DEC_PAGES, n_pages)
    assert n_pages % pages == 0 and page % LANES == 0
    seq3 = lambda i, c, pt: (i, 0, 0)
    grid_spec = pltpu.PrefetchScalarGridSpec(
        num_scalar_prefetch=1,
        grid=(db, n_pages // pages),
        in_specs=[pl.BlockSpec((1, 1, SB_W), seq3), pl.BlockSpec(memory_space=pl.ANY)],
        out_specs=pl.BlockSpec((1, 1, SB_W), seq3),
        scratch_shapes=[pltpu.VMEM((2, pages, 2 * SB_W, page), F32), pltpu.SemaphoreType.DMA((2,)),
                        pltpu.VMEM((SB_HEADS, SB_W), F32), pltpu.VMEM((SB_HEADS, 1), F32)],
    )
    return pl.pallas_call(
        functools.partial(_sb_dec_kernel, pages=pages),
        grid_spec=grid_spec,
        out_shape=jax.ShapeDtypeStruct((db, 1, SB_W), BF16),
        compiler_params=_cparams(("arbitrary", "arbitrary")),
        name="sb_decode",
    )(page_table, q, cache_feat)


def _rope_tables(pos):
    half = HEAD_DIM // 2
    inv = jnp.power(ROPE_THETA, -jnp.arange(half, dtype=F32) * (2.0 / HEAD_DIM))
    ang = pos.astype(F32)[:, None] * inv[None, :]
    cos, sin = jnp.cos(ang), jnp.sin(ang)
    reps = LANES // HEAD_DIM
    return jnp.tile(cos, (1, 2 * reps)), jnp.tile(jnp.concatenate([-sin, sin], axis=1), (1, reps))


def _cmp_to_sel(n_cmp, n_sel, rows, cols):
    s = np.arange(n_cmp)[:, None] * D_CMP
    b = np.arange(n_sel)[None, :] * L_SEL
    m = np.zeros((rows, cols), np.float32)
    m[:n_cmp, :n_sel] = (s < b + L_SEL) & (s + L_CMP > b)
    return jnp.asarray(m)


def _feature_major(a):
    n, tok = a.shape[:2]
    return jnp.transpose(a, (0, 2, 3, 4, 1)).reshape(n, -1, tok)


def _token_major(a, feat_shape):
    n, _, tok = a.shape
    return jnp.transpose(a.reshape((n,) + tuple(feat_shape) + (tok,)), (0, 4, 1, 2, 3))


def _round_up(a, b):
    return -(-a // b) * b


def _row_tile(n):
    for tm in (512, 256, 128, 64, 32, 16, 8):
        if n % tm == 0:
            return tm
    raise ValueError(f"row count {n} is not a multiple of 8")


def kernel(x_prompt, x_sample, p_prompt, p_sample, cache_nsa, cache_sb, state_win, page_table, w_in, w_cmp, pe_cmp,
           w_oa, w_ob, w_out, g_pre_mix, g_post_mix, g_pre_ffn, g_post_ffn, w_up, w_down, w_ple, w_ple_gate):
    B, T, D = x_prompt.shape
    DB, TS = x_sample.shape[:2]
    assert TS == 1, "the decode kernels take one new token per sequence"
    depth = w_in.shape[0]
    n_phys, page = cache_nsa.shape[1:3]
    n_pages = page_table.shape[1]
    past = n_pages * page
    w_buf = state_win.shape[2]
    assert T % TOK_PER_ROW == 0 and page % TOK_PER_ROW == 0 and w_buf >= 2
    n2_p = T // TOK_PER_ROW
    n2_s = past // TOK_PER_ROW
    feat = 4 * NSA_KVW

    cos_p, sin_p = _rope_tables(jnp.arange(T, dtype=jnp.int32))
    cos_s, sin_s = _rope_tables(jnp.full((DB,), past, jnp.int32))
    wsel_p = _round_up(T // L_SEL, LANES)
    c2s_p = _cmp_to_sel(n2_p - 1, T // L_SEL, n2_p, wsel_p)
    n_sel_s = past // L_SEL + 1
    wsel_s = _round_up(n_sel_s, LANES)
    c2s_s = _cmp_to_sel(n2_s - 1, n_sel_s, n2_s, wsel_s)
    et = jnp.asarray(np.where(np.arange(T)[:, None] // L_SEL == np.arange(wsel_p)[None, :], NEG_INF, 0.0), BF16)
    tri = jnp.asarray(np.arange(SB_T)[:, None] > np.arange(SB_T)[None, :], BF16)

    tm_p = _row_tile(B * T)
    tm_s = _row_tile(DB)
    xp = x_prompt.reshape(B * T, D)
    xs = x_sample.reshape(DB, D)
    o = np.cumsum((0, NSA_QW, 6 * NSA_KVW, GATE_W, 3 * SB_W, 2 * D))
    nsa_p, sb_p, win_p, nsa_s, sb_s, win_s = [], [], [], [], [], []
    for i in range(depth):
        w = w_in[i]
        w_proj = jnp.concatenate([w[:, o[0]:o[2]], w[:, o[3]:o[4]],
                                  jnp.pad(w[:, o[2]:o[3]], ((0, 0), (0, LANES - GATE_W)))], axis=1).astype(BF16)
        w_mg = w[:, o[4]:o[5]].astype(BF16)
        wc = jnp.repeat(w_cmp[i], NSA_KV, axis=0)
        wblk = jnp.einsum('ab,alde->ladbe', jnp.eye(2 * NSA_KV, dtype=F32), wc).reshape(L_CMP, 2 * NSA_KVW, 2 * NSA_KVW).astype(BF16)
        wlo, whi = wblk[:TOK_PER_ROW], wblk[TOK_PER_ROW:]
        pe = jnp.broadcast_to(pe_cmp[i][:, :, None, :], (L_CMP, 2, NSA_KV, HEAD_DIM)).reshape(L_CMP, 1, 2 * NSA_KVW)
        pelo, pehi = pe[:TOK_PER_ROW], pe[TOK_PER_ROW:]
        g_pre = g_pre_mix[i].reshape(1, D)
        tail_w = (g_pre, w_mg, w_oa[i].astype(BF16), w_ob[i].astype(BF16), w_out[i].astype(BF16), g_post_mix[i].reshape(1, D))
        ffn_w = (g_pre_ffn[i].reshape(1, D), w_up[i].astype(BF16), w_down[i].astype(BF16), g_post_ffn[i].reshape(1, D),
                 w_ple[i].astype(BF16), w_ple_gate[i].astype(BF16))

        (q, nsa, win, gate, sbq, ksel, vsel, kwin, vwin, sbk, sbv, nsa_t, win_t, sbkv_t) = _project(
            xp, g_pre, w_proj, cos_p, sin_p, tm_p)
        b3 = lambda a: a.reshape(B, T, a.shape[-1])
        kc = _compress(nsa.reshape(B, n2_p, TOK_PER_ROW * feat), wlo, whi, pelo, pehi, min(128, n2_p))
        o_a = _nsa_prompt(b3(q), b3(gate), kc, b3(ksel), b3(vsel), b3(kwin), b3(vwin), c2s_p, et)
        o_b = _sb_prompt(b3(sbq), b3(sbk), b3(sbv), tri)
        xp = _mix(xp, o_a.reshape(B * T, NSA_QW), o_b.reshape(B * T, SB_W), *tail_w, tm_p)
        xp = _ffn(xp, p_prompt[i].reshape(B * T, -1), *ffn_w, tm_p)
        nsa_p.append(_token_major(nsa_t, (4, NSA_KV, HEAD_DIM)))
        sb_p.append(_token_major(sbkv_t, (2, SB_HEADS, HEAD_DIM)))
        win_p.append(_token_major(win_t[:, :, T - min(WINDOW, T):], (2, NSA_KV, HEAD_DIM)))

        (q, nsa, win, gate, sbq, _, _, _, _, _, _, nsa_t, _, sbkv_t) = _project(xs, g_pre, w_proj, cos_s, sin_s, tm_s)
        r3 = lambda a: a.reshape(DB, 1, a.shape[-1])
        o_a, win_new = _nsa_decode(page_table, r3(q), r3(gate), r3(nsa), r3(win),
                                   _feature_major(state_win[i]), _feature_major(cache_nsa[i]),
                                   wlo, whi, pelo, pehi, c2s_s, past)
        o_b = _sb_decode(page_table, r3(sbq), _feature_major(cache_sb[i]))
        xs = _mix(xs, o_a.reshape(DB, NSA_QW), o_b.reshape(DB, SB_W), *tail_w, tm_s)
        xs = _ffn(xs, p_sample[i].reshape(DB, -1), *ffn_w, tm_s)
        nsa_s.append(jnp.swapaxes(_token_major(nsa_t, (4, NSA_KV, HEAD_DIM)), 0, 1))
        sb_s.append(jnp.swapaxes(_token_major(sbkv_t, (2, SB_HEADS, HEAD_DIM)), 0, 1))
        win_s.append(jnp.transpose(win_new.reshape(DB, 2, NSA_KV, HEAD_DIM, w_buf), (0, 4, 1, 2, 3)))
    return (xp.reshape(B, T, D), xs.reshape(DB, TS, D), jnp.stack(nsa_p), jnp.stack(sb_p), jnp.stack(win_p),
            jnp.stack(nsa_s), jnp.stack(sb_s), jnp.stack(win_s))
```

```python
import functools
import math

import numpy as np
import jax
import jax.numpy as jnp
from jax import lax
from jax.experimental import pallas as pl
from jax.experimental.pallas import tpu as pltpu

F32 = jnp.float32
BF16 = jnp.bfloat16

HEAD_DIM = 64
NSA_HEADS = 8
NSA_KV = 2
NSA_GROUP = NSA_HEADS // NSA_KV
SB_HEADS = 8
L_CMP = 32
D_CMP = 16
L_SEL = 64
N_SEL = 16
N_LOCAL = 2
WINDOW = 512
ROPE_THETA = 10000.0
EPS = 1e-6
FORCE_BONUS = 1e4
NEG_INF = -1e30
LOWEST = -3e38
SCALE = HEAD_DIM ** -0.5
LOG2E = 1.4426950408889634
NSA_QW = NSA_HEADS * HEAD_DIM
NSA_KVW = NSA_KV * HEAD_DIM
SB_W = SB_HEADS * HEAD_DIM
GATE_W = 3 * NSA_HEADS

LANES = 128
TOK_PER_ROW = 16
VMEM_LIMIT = 56 * 1024 * 1024


def _cparams(sem):
    return pltpu.CompilerParams(dimension_semantics=sem, vmem_limit_bytes=VMEM_LIMIT)


def _dot(a, b):
    return jnp.dot(a, b, preferred_element_type=F32)


def _nt(a, b):
    return lax.dot_general(a, b, (((1,), (1,)), ((), ())), preferred_element_type=F32)


def _row_dot(a, row):
    return jnp.sum(a.astype(F32) * row.astype(BF16).astype(F32), axis=1, keepdims=True)


def _rms(x, g):
    return x * lax.rsqrt(jnp.mean(x * x, axis=-1, keepdims=True) + EPS) * g


def _sigmoid(x):
    return 1.0 / (1.0 + jnp.exp(-x))


def _masked_softmax(s, mask):
    s = jnp.where(mask, s, NEG_INF)
    e = jnp.exp(s - jnp.max(s, axis=-1, keepdims=True))
    return e / jnp.sum(e, axis=-1, keepdims=True)


def _proj_kernel(x_ref, g_ref, w_ref, cos_ref, sin_ref,
                 q_ref, nsa_ref, win_ref, gate_ref, sbq_ref,
                 ksel_ref, vsel_ref, kwin_ref, vwin_ref, sbk_ref, sbv_ref, nsat_ref, wint_ref, sbkvt_ref):
    h = _rms(x_ref[...], g_ref[...]).astype(BF16)
    cos = cos_ref[...]
    sin = sin_ref[...]
    lane = lax.broadcasted_iota(jnp.int32, cos.shape, 1)
    first_half = (lane & (HEAD_DIM - 1)) < HEAD_DIM // 2

    def rope(z):
        other = jnp.where(first_half, pltpu.roll(z, LANES - HEAD_DIM // 2, 1), pltpu.roll(z, HEAD_DIM // 2, 1))
        return z * cos + other * sin

    def blk(z, c):
        return z[:, c * LANES:(c + 1) * LANES]

    zq = _dot(h, w_ref[:, 0:NSA_QW])
    q = jnp.concatenate([rope(blk(zq, c)) for c in range(NSA_QW // LANES)], axis=1)
    q_ref[...] = (q * SCALE).astype(BF16)

    o = NSA_QW
    zkv = _dot(h, w_ref[:, o:o + 6 * NSA_KVW])
    kc, vc, ks, vs, kw, vw = rope(blk(zkv, 0)), blk(zkv, 1), rope(blk(zkv, 2)), blk(zkv, 3), rope(blk(zkv, 4)), blk(zkv, 5)
    nsa_ref[...] = jnp.concatenate([kc, vc, ks, vs], axis=1)
    win_ref[...] = jnp.concatenate([kw, vw], axis=1)
    for c, z in enumerate((kc, vc, ks, vs)):
        nsat_ref[0, c * LANES:(c + 1) * LANES, :] = z.T
    for c, z in enumerate((kw, vw)):
        wint_ref[0, c * LANES:(c + 1) * LANES, :] = z.T
    ksel_ref[...] = ks.astype(BF16)
    vsel_ref[...] = vs.astype(BF16)
    kwin_ref[...] = kw.astype(BF16)
    vwin_ref[...] = vw.astype(BF16)

    o += 6 * NSA_KVW
    zsb = _dot(h, w_ref[:, o:o + 3 * SB_W])
    sbq_ref[...] = (zsb[:, 0:SB_W] * SCALE).astype(BF16)
    for c in range(2 * SB_W // LANES):
        sbkvt_ref[0, c * LANES:(c + 1) * LANES, :] = zsb[:, SB_W + c * LANES:SB_W + (c + 1) * LANES].T
    sbk_ref[...] = zsb[:, SB_W:2 * SB_W].astype(BF16)
    sbv_ref[...] = zsb[:, 2 * SB_W:3 * SB_W].astype(BF16)

    o += 3 * SB_W
    gate_ref[...] = _dot(h, w_ref[:, o:o + LANES])


def _project(x2d, g, w_proj, cos, sin, tm):
    n, d = x2d.shape
    n_tab = cos.shape[0] // tm
    row = lambda i: (i, 0)
    const = lambda i: (0, 0)
    tab = lambda i: (i % n_tab, 0)
    widths = [(NSA_QW, BF16), (4 * NSA_KVW, F32), (2 * NSA_KVW, F32), (LANES, F32), (SB_W, BF16),
              (NSA_KVW, BF16), (NSA_KVW, BF16), (NSA_KVW, BF16), (NSA_KVW, BF16), (SB_W, BF16), (SB_W, BF16)]
    t_feats = [4 * NSA_KVW, 2 * NSA_KVW, 2 * SB_W]
    grp = lambda i: (i // n_tab, 0, i % n_tab)
    return pl.pallas_call(
        _proj_kernel,
        grid=(n // tm,),
        in_specs=[pl.BlockSpec((tm, d), row), pl.BlockSpec((1, d), const),
                  pl.BlockSpec(w_proj.shape, const, pipeline_mode=pl.Buffered(1)),
                  pl.BlockSpec((tm, LANES), tab), pl.BlockSpec((tm, LANES), tab)],
        out_specs=[pl.BlockSpec((tm, wd), row) for wd, _ in widths] + [pl.BlockSpec((1, f, tm), grp) for f in t_feats],
        out_shape=[jax.ShapeDtypeStruct((n, wd), dt) for wd, dt in widths]
        + [jax.ShapeDtypeStruct((n // (n_tab * tm), f, n_tab * tm), F32) for f in t_feats],
        compiler_params=_cparams(("parallel",)),
        name="proj",
    )(x2d, g, w_proj, cos, sin)


def _compress_rows(x_of, wlo_ref, whi_ref, pelo_ref, pehi_ref, rows):
    ylo = jnp.zeros((rows, 2 * NSA_KVW), F32)
    yhi = jnp.zeros((rows, 2 * NSA_KVW), F32)
    for l in range(TOK_PER_ROW):
        xl = x_of(l)
        ylo = ylo + _dot((xl + pelo_ref[l]).astype(BF16), wlo_ref[l])
        yhi = yhi + _dot((xl + pehi_ref[l]).astype(BF16), whi_ref[l])
    return ylo, yhi


def _compress_kernel(x_ref, wlo_ref, whi_ref, pelo_ref, pehi_ref, out_ref, ylo_ref, yhi_ref):
    t = pl.program_id(1)
    rows = x_ref.shape[1]
    n2 = ylo_ref.shape[0]
    feat = 4 * NSA_KVW
    ylo, yhi = _compress_rows(lambda l: x_ref[0, :, l * feat:l * feat + 2 * NSA_KVW],
                              wlo_ref, whi_ref, pelo_ref, pehi_ref, rows)
    r0 = pl.multiple_of(t * rows, rows)
    ylo_ref[pl.ds(r0, rows), :] = ylo
    yhi_ref[pl.ds(r0, rows), :] = yhi

    @pl.when(t == pl.num_programs(1) - 1)
    def _():
        out_ref[0] = (ylo_ref[...] + pltpu.roll(yhi_ref[...], n2 - 1, 0)).astype(BF16)


def _compress(nsa_rows, wlo, whi, pelo, pehi, rows):
    b, n2, wd = nsa_rows.shape
    c3 = lambda i, t: (0, 0, 0)
    return pl.pallas_call(
        _compress_kernel,
        grid=(b, n2 // rows),
        in_specs=[pl.BlockSpec((1, rows, wd), lambda i, t: (i, t, 0)),
                  pl.BlockSpec(wlo.shape, c3), pl.BlockSpec(whi.shape, c3),
                  pl.BlockSpec(pelo.shape, c3), pl.BlockSpec(pehi.shape, c3)],
        out_specs=pl.BlockSpec((1, n2, 2 * NSA_KVW), lambda i, t: (i, 0, 0)),
        out_shape=jax.ShapeDtypeStruct((b, n2, 2 * NSA_KVW), BF16),
        scratch_shapes=[pltpu.VMEM((n2, 2 * NSA_KVW), F32), pltpu.VMEM((n2, 2 * NSA_KVW), F32)],
        compiler_params=_cparams(("parallel", "arbitrary")),
        name="compress",
    )(nsa_rows, wlo, whi, pelo, pehi)


def _select_blocks(imp, tq, n_sel):
    w = imp.shape[1]
    blk = lax.broadcasted_iota(jnp.int32, (1, w), 1)
    cur = tq // L_SEL
    forced = (blk == 0) | ((blk <= cur) & (blk > cur - N_LOCAL))
    valid = (blk * L_SEL <= tq) & (blk < n_sel)
    x = jnp.where(valid, jnp.where(forced, imp + FORCE_BONUS, imp), NEG_INF)
    x = jnp.where(blk < n_sel, x, LOWEST)
    sel = jnp.zeros(x.shape, jnp.bool_)
    for _ in range(min(N_SEL, n_sel)):
        m = jnp.max(x, axis=1, keepdims=True)
        first = jnp.min(jnp.where(x == m, blk, w), axis=1, keepdims=True)
        hit = blk == first
        sel = sel | hit
        x = jnp.where(hit, LOWEST, x)
    return sel & valid


def _select_blocks_t(imp_t, tq_row, n_sel):
    w = imp_t.shape[0]
    blk = lax.broadcasted_iota(jnp.int32, (w, 1), 0)
    cur = tq_row // L_SEL
    forced = (blk == 0) | ((blk <= cur) & (blk > cur - N_LOCAL))
    valid = (blk * L_SEL <= tq_row) & (blk < n_sel)
    x = jnp.where(valid, jnp.where(forced, imp_t + FORCE_BONUS, imp_t), NEG_INF)
    x = jnp.where(blk < n_sel, x, LOWEST)
    for _ in range(min(N_SEL, n_sel)):
        m = jnp.max(x, axis=0, keepdims=True)
        first = jnp.min(jnp.where(x == m, blk, w), axis=0, keepdims=True)
        x = jnp.where(blk == first, LOWEST, x)
    return (x == LOWEST) & valid


NSA_TQ = 128
NSA_KC = 1024


def _nsa_kernel(q_ref, gate_ref, kc_ref, ksel_ref, vsel_ref, kwin_ref, vwin_ref, c2s_ref, et_ref, o_ref, *, seq):
    i = pl.program_id(1)
    tq_n = NSA_TQ
    q0 = i * tq_n
    n2 = kc_ref.shape[1]
    n_cmp = n2 - 1
    n_sel = seq // L_SEL
    wsel = c2s_ref.shape[1]
    rows = NSA_HEADS * tq_n

    qf = q_ref[0].astype(F32)
    gate = _sigmoid(gate_ref[0])
    lane = lax.broadcasted_iota(jnp.int32, (tq_n, LANES), 1)
    tq = q0 + lax.broadcasted_iota(jnp.int32, (tq_n, 1), 0)
    t8 = jnp.concatenate([tq] * NSA_HEADS, axis=0)
    tq_row = q0 + lax.broadcasted_iota(jnp.int32, (1, tq_n), 1)
    blkid = lax.broadcasted_iota(jnp.int32, (1, wsel), 1)

    def per_head(by_group):
        return jnp.concatenate([by_group[hd // NSA_GROUP] for hd in range(NSA_HEADS)], axis=0)

    def online(carry, s, v):
        m, l, acc = carry
        mn = jnp.maximum(m, jnp.max(s, axis=1, keepdims=True))
        a = jnp.exp(m - mn)
        p = jnp.exp(s - mn)
        return mn, a * l + jnp.sum(p, axis=1, keepdims=True), a * acc + _dot(p.astype(BF16), v)

    embs = []
    for hd in range(NSA_HEADS):
        g = hd // NSA_GROUP
        b = qf[:, (hd // 2) * LANES:(hd // 2 + 1) * LANES]
        if hd % 2 != g:
            b = pltpu.roll(b, HEAD_DIM, 1)
        embs.append(jnp.where((lane // HEAD_DIM) == g, b, 0.0))
    q8 = jnp.concatenate(embs, axis=0).astype(BF16)

    s = _nt(q8, kc_ref[0, :, 0:LANES])
    ncol = lax.broadcasted_iota(jnp.int32, (1, n2), 1)
    mc = (ncol * D_CMP + (L_CMP - 1) <= t8) & (ncol < n_cmp)
    p = jnp.where(mc, _masked_softmax(s, mc), 0.0)
    o_cmp = _dot(p.astype(BF16), kc_ref[0, :, LANES:2 * LANES])
    psums = []
    for g in range(NSA_KV):
        blocks = [p[hd * tq_n:(hd + 1) * tq_n] for hd in range(g * NSA_GROUP, (g + 1) * NSA_GROUP)]
        psums.append(functools.reduce(lambda a, b: a + b, blocks))
    imp = jnp.dot(jnp.concatenate(psums, axis=0), c2s_ref[...], preferred_element_type=F32,
                  precision=lax.Precision.HIGHEST)

    sel_t = _select_blocks_t(imp.T, jnp.concatenate([tq_row] * NSA_KV, axis=1), n_sel)
    notsel = jnp.where(sel_t, 0.0, 1.0).T
    notsel_g = [notsel[g * tq_n:(g + 1) * tq_n] for g in range(NSA_KV)]
    before_diag = [jnp.where(blkid >= q0 // L_SEL, 1.0, ns) for ns in notsel_g]
    aug_main = jnp.concatenate([q8, per_head(before_diag).astype(BF16)], axis=1)
    aug_diag = jnp.concatenate([q8, per_head(notsel_g).astype(BF16)], axis=1)

    def main_body(c, carry):
        k0 = pl.multiple_of(c * NSA_KC, NSA_KC)
        rhs = jnp.concatenate([ksel_ref[0, pl.ds(k0, NSA_KC), :], et_ref[pl.ds(k0, NSA_KC), :]], axis=1)
        return online(carry, _nt(aug_main, rhs), vsel_ref[0, pl.ds(k0, NSA_KC), :])

    init = (jnp.full((rows, 1), LOWEST, F32), jnp.zeros((rows, 1), F32), jnp.zeros((rows, LANES), F32))
    carry = lax.fori_loop(0, (q0 + NSA_KC - 1) // NSA_KC, main_body, init)
    kd = pl.multiple_of(q0, tq_n)
    rhs = jnp.concatenate([ksel_ref[0, pl.ds(kd, tq_n), :], et_ref[pl.ds(kd, tq_n), :]], axis=1)
    s = jnp.where(tq_row <= t8, _nt(aug_diag, rhs), NEG_INF)
    _, l, acc = online(carry, s, vsel_ref[0, pl.ds(kd, tq_n), :])
    o_sel = acc / l

    wlen = WINDOW + tq_n
    w0 = pl.multiple_of(jnp.maximum(q0 - WINDOW, 0), tq_n)
    wpos = w0 + lax.broadcasted_iota(jnp.int32, (1, wlen), 1)
    mw = (wpos <= t8) & (wpos > t8 - WINDOW)
    s = _nt(q8, kwin_ref[0, pl.ds(w0, wlen), :])
    o_w = _dot(_masked_softmax(s, mw).astype(BF16), vwin_ref[0, pl.ds(w0, wlen), :])

    outs = []
    for hd in range(NSA_HEADS):
        r = slice(hd * tq_n, (hd + 1) * tq_n)
        o = (gate[:, 3 * hd:3 * hd + 1] * o_cmp[r] + gate[:, 3 * hd + 1:3 * hd + 2] * o_sel[r]
             + gate[:, 3 * hd + 2:3 * hd + 3] * o_w[r])
        if hd % 2 != hd // NSA_GROUP:
            o = pltpu.roll(o, HEAD_DIM, 1)
        outs.append(o)
    pairs = [jnp.where(lane < HEAD_DIM, outs[2 * p], outs[2 * p + 1]) for p in range(NSA_HEADS // 2)]
    o_ref[0] = jnp.concatenate(pairs, axis=1).astype(BF16)


def _nsa_prompt(q, gate, kc, ksel, vsel, kwin, vwin, c2s, et):
    b, t, _ = q.shape
    assert t % NSA_KC == 0 and t >= WINDOW + NSA_TQ
    blk = lambda i, j: (i, j, 0)
    per_b = lambda i, j: (i, 0, 0)
    c2 = lambda i, j: (0, 0)
    kv_spec = pl.BlockSpec((1, t, LANES), per_b)
    return pl.pallas_call(
        functools.partial(_nsa_kernel, seq=t),
        grid=(b, t // NSA_TQ),
        in_specs=[pl.BlockSpec((1, NSA_TQ, NSA_QW), blk), pl.BlockSpec((1, NSA_TQ, LANES), blk),
                  pl.BlockSpec((1,) + kc.shape[1:], per_b), kv_spec, kv_spec, kv_spec, kv_spec,
                  pl.BlockSpec(c2s.shape, c2), pl.BlockSpec(et.shape, c2)],
        out_specs=pl.BlockSpec((1, NSA_TQ, NSA_QW), blk),
        out_shape=jax.ShapeDtypeStruct((b, t, NSA_QW), BF16),
        compiler_params=_cparams(("parallel", "arbitrary")),
        name="nsa_prompt",
    )(q, gate, kc, ksel, vsel, kwin, vwin, c2s, et)


SB_T = 256
SB_TQ = 512


def _softplus2(z2):
    return jnp.maximum(z2, 0.0) + jnp.log2(1.0 + jnp.exp2(-jnp.abs(z2)))


def _sb_kernel(q_ref, k_ref, v_ref, u_ref, o_ref):
    i = pl.program_id(2)
    tq, kc = SB_TQ, SB_T
    n_diag = tq // kc
    q = q_ref[0].astype(F32)
    lane = lax.broadcasted_iota(jnp.int32, (tq, LANES), 1)
    u = u_ref[...]
    qrow = lax.broadcasted_iota(jnp.int32, (tq, 1), 0)
    qpos = i * tq + jnp.concatenate([qrow] * 2, axis=0)
    kcol = lax.broadcasted_iota(jnp.int32, (1, kc), 1)
    q2 = jnp.concatenate([jnp.where((lane // HEAD_DIM) == hh, q, 0.0) for hh in range(2)], axis=0).astype(BF16)

    def scores(c):
        k0 = pl.multiple_of(c * kc, kc)
        return _nt(q2, k_ref[0, pl.ds(k0, kc), :]) * LOG2E

    def chunk(c, z, carry, acc, diagonal):
        sp = _softplus2(z)
        lm = -sp
        if diagonal:
            below = c * kc + kcol < qpos
            lm = jnp.where(below, lm, 0.0)
        after = _dot(lm.astype(BF16), u) + carry
        a = jnp.exp2(z - sp + after)
        if diagonal:
            a = jnp.where(below, a, 0.0)
        k0 = pl.multiple_of(c * kc, kc)
        acc = acc + _dot(a.astype(BF16), v_ref[0, pl.ds(k0, kc), :])
        return carry + jnp.sum(lm, axis=1, keepdims=True), acc

    first = i * n_diag

    def body(n, state):
        z, carry, acc = state
        c = first - 1 - n
        z_next = scores(jnp.maximum(c - 1, 0))
        carry, acc = chunk(c, z, carry, acc, False)
        return z_next, carry, acc

    carry, acc = jnp.zeros((2 * tq, 1), F32), jnp.zeros((2 * tq, LANES), F32)
    for d in reversed(range(n_diag)):
        carry, acc = chunk(first + d, scores(first + d), carry, acc, True)
    _, _, acc = lax.fori_loop(0, first, body, (scores(jnp.maximum(first - 1, 0)), carry, acc))
    o_ref[0] = jnp.where(lane < HEAD_DIM, acc[0:tq], acc[tq:2 * tq]).astype(BF16)


def _sb_prompt(q, k, v, u):
    b, t, wd = q.shape
    assert t % SB_TQ == 0 and SB_TQ % SB_T == 0
    blk = lambda i, p, j: (i, j, p)
    per = lambda i, p, j: (i, 0, p)
    return pl.pallas_call(
        _sb_kernel,
        grid=(b, wd // LANES, t // SB_TQ),
        in_specs=[pl.BlockSpec((1, SB_TQ, LANES), blk), pl.BlockSpec((1, t, LANES), per),
                  pl.BlockSpec((1, t, LANES), per), pl.BlockSpec(u.shape, lambda i, p, j: (0, 0))],
        out_specs=pl.BlockSpec((1, SB_TQ, LANES), blk),
        out_shape=jax.ShapeDtypeStruct((b, t, wd), BF16),
        compiler_params=_cparams(("parallel", "parallel", "arbitrary")),
        name="sb_prompt",
    )(q, k, v, u)


def _mix_kernel(x_ref, oa_ref, ob_ref, gpre_ref, wmg_ref, woa_ref, wob_ref, wout_ref, gpost_ref, o_ref):
    x = x_ref[...]
    d = x.shape[1]
    h = _rms(x, gpre_ref[...]).astype(BF16)
    mg = _dot(h, wmg_ref[...])
    ya = _dot(oa_ref[...], woa_ref[...])
    yb = _dot(ob_ref[...], wob_ref[...])
    merged = _sigmoid(mg[:, 0:d]) * ya + _sigmoid(mg[:, d:2 * d]) * yb
    o_ref[...] = x + _rms(_dot(merged.astype(BF16), wout_ref[...]), gpost_ref[...])


def _mix(x2d, oa, ob, gpre, wmg, woa, wob, wout, gpost, tm):
    n, d = x2d.shape
    row = lambda i: (i, 0)
    const = lambda i: (0, 0)
    wspec = lambda w: pl.BlockSpec(w.shape, const, pipeline_mode=pl.Buffered(1))
    return pl.pallas_call(
        _mix_kernel,
        grid=(n // tm,),
        in_specs=[pl.BlockSpec((tm, d), row), pl.BlockSpec((tm, oa.shape[1]), row), pl.BlockSpec((tm, ob.shape[1]), row),
                  pl.BlockSpec((1, d), const), wspec(wmg), wspec(woa), wspec(wob), wspec(wout), pl.BlockSpec((1, d), const)],
        out_specs=pl.BlockSpec((tm, d), row),
        out_shape=jax.ShapeDtypeStruct((n, d), F32),
        compiler_params=_cparams(("parallel",)),
        name="mix",
    )(x2d, oa, ob, gpre, wmg, woa, wob, wout, gpost)


FFN_CHUNK = 1024


def _ffn_kernel(x_ref, p_ref, gpre_ref, wup_ref, wdown_ref, gpost_ref, wple_ref, wpg_ref, o_ref):
    x = x_ref[...]
    h = _rms(x, gpre_ref[...]).astype(BF16)
    f = jnp.zeros(x.shape, F32)
    for c in range(wup_ref.shape[1] // FFN_CHUNK):
        cs = slice(c * FFN_CHUNK, (c + 1) * FFN_CHUNK)
        up = jnp.maximum(_dot(h, wup_ref[:, cs]), 0.0)
        f = f + _dot((up * up).astype(BF16), wdown_ref[cs, :])
    x = x + _rms(f, gpost_ref[...])
    ple = _dot(p_ref[...].astype(BF16), wple_ref[...])
    o_ref[...] = x + ple * _sigmoid(_dot(x.astype(BF16), wpg_ref[...]))


def _ffn(x2d, p2d, gpre, wup, wdown, gpost, wple, wpg, tm):
    n, d = x2d.shape
    row = lambda i: (i, 0)
    const = lambda i: (0, 0)
    wspec = lambda w: pl.BlockSpec(w.shape, const, pipeline_mode=pl.Buffered(1))
    return pl.pallas_call(
        _ffn_kernel,
        grid=(n // tm,),
        in_specs=[pl.BlockSpec((tm, d), row), pl.BlockSpec((tm, p2d.shape[1]), row), pl.BlockSpec((1, d), const),
                  wspec(wup), wspec(wdown), pl.BlockSpec((1, d), const), wspec(wple), wspec(wpg)],
        out_specs=pl.BlockSpec((tm, d), row),
        out_shape=jax.ShapeDtypeStruct((n, d), F32),
        compiler_params=_cparams(("parallel",)),
        name="ffn",
    )(x2d, p2d, gpre, wup, wdown, gpost, wple, wpg)


def _nsa_dec_kernel(pt_ref, q_ref, gate_ref, new_ref, wnew_ref, state_ref, cache_ref,
                    wlo_ref, whi_ref, pelo_ref, pehi_ref, c2s_ref,
                    o_ref, win_ref, buf, sem, tok_ref, *, n_pages, past):
    s_id = pl.program_id(0)
    n_seq = pl.num_programs(0)
    slot = s_id % 2
    page = buf.shape[3]
    n2 = tok_ref.shape[1] // TOK_PER_ROW
    n_cmp = n2 - 1
    n_sel = past // L_SEL + 1
    w_buf = state_ref.shape[2]
    cmp_w = 2 * NSA_KVW

    def page_copy(seq, p, sl):
        return pltpu.make_async_copy(cache_ref.at[pt_ref[seq, p]], buf.at[sl, p], sem.at[sl])

    def start_all(seq, sl):
        lax.fori_loop(0, n_pages, lambda p, c: (page_copy(seq, p, sl).start(), c)[1], 0)

    @pl.when(s_id == 0)
    def _():
        start_all(0, 0)

    @pl.when(s_id + 1 < n_seq)
    def _():
        start_all(s_id + 1, 1 - slot)

    lax.fori_loop(0, n_pages, lambda p, c: (page_copy(s_id, p, slot).wait(), c)[1], 0)

    halves = range(cmp_w // LANES)

    def to_token_major(p, c):
        r0 = pl.multiple_of(p * page, page)
        for h in halves:
            tok_ref[h, pl.ds(r0, page), :] = buf[slot, p, h * LANES:(h + 1) * LANES, :].T
        return c

    lax.fori_loop(0, n_pages, to_token_major, 0, unroll=math.gcd(n_pages, 8))
    ylo, yhi = _compress_rows(
        lambda l: jnp.concatenate([tok_ref[h, pl.ds(l, n2, stride=TOK_PER_ROW), :] for h in halves], axis=1),
        wlo_ref, whi_ref, pelo_ref, pehi_ref, n2)
    kc = (ylo + pltpu.roll(yhi, n2 - 1, 0)).astype(BF16)

    qf = q_ref[0].astype(F32)
    lane1 = lax.broadcasted_iota(jnp.int32, (1, LANES), 1)
    rows = []
    for hd in range(NSA_HEADS):
        g = hd // NSA_GROUP
        b = qf[:, (hd // 2) * LANES:(hd // 2 + 1) * LANES]
        if hd % 2 != g:
            b = pltpu.roll(b, HEAD_DIM, 1)
        rows.append(jnp.where((lane1 // HEAD_DIM) == g, b, 0.0))
    q8 = jnp.concatenate(rows, axis=0).astype(BF16)
    head_row = lax.broadcasted_iota(jnp.int32, (NSA_HEADS, 1), 0)
    tq = jnp.full((NSA_KV, 1), past, jnp.int32)

    s = _nt(q8, kc[:, 0:LANES])
    ncol = lax.broadcasted_iota(jnp.int32, (1, n2), 1)
    mc = (ncol * D_CMP + (L_CMP - 1) <= past) & (ncol < n_cmp)
    p = jnp.where(mc, _masked_softmax(s, mc), 0.0)
    o_cmp = _dot(p.astype(BF16), kc[:, LANES:2 * LANES])
    psum = jnp.concatenate([jnp.sum(p[g * NSA_GROUP:(g + 1) * NSA_GROUP], axis=0, keepdims=True)
                            for g in range(NSA_KV)], axis=0)
    imp = jnp.dot(psum, c2s_ref[...], preferred_element_type=F32, precision=lax.Precision.HIGHEST)

    sel = jnp.where(_select_blocks(imp, tq, n_sel), 1.0, 0.0)
    lane8 = lax.broadcasted_iota(jnp.int32, (NSA_HEADS, LANES), 1)
    blocks_per_page = page // L_SEL
    page_tok = lax.broadcasted_iota(jnp.int32, (1, page), 1)

    def page_mask(p):
        rows = []
        for g in range(NSA_KV):
            col = lambda j: sel[g:g + 1, p * blocks_per_page + j:p * blocks_per_page + j + 1]
            m = col(blocks_per_page - 1)
            for j in reversed(range(blocks_per_page - 1)):
                m = jnp.where(page_tok < (j + 1) * L_SEL, col(j), m)
            rows += [jnp.broadcast_to(m, (1, page))] * NSA_GROUP
        return jnp.concatenate(rows, axis=0) > 0.5

    new = new_ref[0]
    s_page = [jnp.where(page_mask(p), _dot(q8, buf[slot, p, 2 * NSA_KVW:3 * NSA_KVW, :].astype(BF16)), NEG_INF)
              for p in range(n_pages)]
    s_new = _row_dot(q8, new[:, 2 * NSA_KVW:3 * NSA_KVW])
    m = s_new
    for sp in s_page:
        m = jnp.maximum(m, jnp.max(sp, axis=1, keepdims=True))
    p_new = jnp.exp(s_new - m)
    den = p_new
    acc = p_new.astype(BF16).astype(F32) * new[:, 3 * NSA_KVW:4 * NSA_KVW].astype(BF16).astype(F32)
    for p, sp in enumerate(s_page):
        pt = jnp.exp(sp - m)
        den = den + jnp.sum(pt, axis=1, keepdims=True)
        acc = acc + _nt(pt.astype(BF16), buf[slot, p, 3 * NSA_KVW:4 * NSA_KVW, :].astype(BF16))
    o_sel = acc / den

    st = state_ref[0]
    wnew = wnew_ref[0]
    s = _dot(q8, st[0:LANES, :].astype(BF16))
    wpos = past - w_buf + lax.broadcasted_iota(jnp.int32, (1, w_buf), 1)
    mw = (wpos > past - WINDOW) & (wpos >= 0)
    s = jnp.where(mw, s, NEG_INF)
    s_new = _row_dot(q8, wnew[:, 0:LANES])
    m = jnp.maximum(s_new, jnp.max(s, axis=1, keepdims=True))
    p_new = jnp.exp(s_new - m)
    pw = jnp.exp(s - m)
    den = p_new + jnp.sum(pw, axis=1, keepdims=True)
    o_w = (_nt(pw.astype(BF16), st[LANES:2 * LANES, :].astype(BF16))
           + p_new.astype(BF16).astype(F32) * wnew[:, LANES:2 * LANES].astype(BF16).astype(F32)) / den

    gate = jnp.broadcast_to(_sigmoid(gate_ref[0]), (NSA_HEADS, LANES))

    def gcol(br):
        return jnp.sum(jnp.where(lane8 == 3 * head_row + br, gate, 0.0), axis=1, keepdims=True)

    o = gcol(0) * o_cmp + gcol(1) * o_sel + gcol(2) * o_w
    placed = []
    for hd in range(NSA_HEADS):
        r = o[hd:hd + 1]
        if hd % 2 != hd // NSA_GROUP:
            r = pltpu.roll(r, HEAD_DIM, 1)
        placed.append(r)
    pairs = [jnp.where(lane1 < HEAD_DIM, placed[2 * p], placed[2 * p + 1]) for p in range(NSA_HEADS // 2)]
    o_ref[0] = jnp.concatenate(pairs, axis=1).astype(BF16)

    nf = st.shape[0]
    on_diag = lax.broadcasted_iota(jnp.int32, (nf, nf), 0) == lax.broadcasted_iota(jnp.int32, (nf, nf), 1)
    new_col = jnp.sum(jnp.where(on_diag, jnp.broadcast_to(wnew, (nf, nf)), 0.0), axis=1, keepdims=True)
    tok = lax.broadcasted_iota(jnp.int32, (1, w_buf), 1)
    win_ref[0] = jnp.where(tok == w_buf - 1, new_col, pltpu.roll(st, w_buf - 1, 1))


def _nsa_decode(page_table, q, gate, new, wnew, state_feat, cache_feat, wlo, whi, pelo, pehi, c2s, past):
    db = q.shape[0]
    n_pages = page_table.shape[1]
    feat, page = cache_feat.shape[1:]
    w_buf = state_feat.shape[2]
    assert page % L_SEL == 0 and page % LANES == 0
    seq3 = lambda i, pt: (i, 0, 0)
    c3 = lambda i, pt: (0, 0, 0)
    c2 = lambda i, pt: (0, 0)
    once = lambda a, imap: pl.BlockSpec(a.shape, imap, pipeline_mode=pl.Buffered(1))
    grid_spec = pltpu.PrefetchScalarGridSpec(
        num_scalar_prefetch=1,
        grid=(db,),
        in_specs=[pl.BlockSpec((1, 1, NSA_QW), seq3), pl.BlockSpec((1, 1, LANES), seq3),
                  pl.BlockSpec((1, 1, 4 * NSA_KVW), seq3), pl.BlockSpec((1, 1, 2 * NSA_KVW), seq3),
                  pl.BlockSpec((1, 2 * NSA_KVW, w_buf), seq3), pl.BlockSpec(memory_space=pl.ANY),
                  once(wlo, c3), once(whi, c3), once(pelo, c3), once(pehi, c3), once(c2s, c2)],
        out_specs=[pl.BlockSpec((1, 1, NSA_QW), seq3), pl.BlockSpec((1, 2 * NSA_KVW, w_buf), seq3)],
        scratch_shapes=[pltpu.VMEM((2, n_pages, feat, page), F32), pltpu.SemaphoreType.DMA((2,)),
                        pltpu.VMEM((2 * NSA_KVW // LANES, n_pages * page, LANES), F32)],
    )
    return pl.pallas_call(
        functools.partial(_nsa_dec_kernel, n_pages=n_pages, past=past),
        grid_spec=grid_spec,
        out_shape=[jax.ShapeDtypeStruct((db, 1, NSA_QW), BF16), jax.ShapeDtypeStruct((db, 2 * NSA_KVW, w_buf), F32)],
        compiler_params=_cparams(("arbitrary",)),
        name="nsa_decode",
    )(page_table, q, gate, new, wnew, state_feat, cache_feat, wlo, whi, pelo, pehi, c2s)


SB_DEC_PAGES = 16


def _suffix_sums(x):
    r, n = x.shape
    lane = lax.broadcasted_iota(jnp.int32, (r, LANES), 1)
    run = jnp.zeros((r, 1), F32)
    outs = [None] * (n // LANES)
    for j in reversed(range(n // LANES)):
        xb = x[:, j * LANES:(j + 1) * LANES]
        inc = xb
        d = 1
        while d < LANES:
            inc = inc + jnp.where(lane + d < LANES, pltpu.roll(inc, LANES - d, 1), 0.0)
            d *= 2
        outs[j] = inc - xb + run
        run = run + inc[:, 0:1]
    return jnp.concatenate(outs, axis=1), run


def _sb_dec_kernel(pt_ref, q_ref, cache_ref, o_ref, buf, sem, acc_ref, carry_ref, *, pages):
    s_id = pl.program_id(0)
    c_id = pl.program_id(1)
    n_seq = pl.num_programs(0)
    n_chunk = pl.num_programs(1)
    step = s_id * n_chunk + c_id
    slot = step % 2

    def page_copy(seq, ch, p, sl):
        src = cache_ref.at[pt_ref[seq, (n_chunk - 1 - ch) * pages + p]]
        return pltpu.make_async_copy(src, buf.at[sl, p], sem.at[sl])

    def start_all(seq, ch, sl):
        for p in range(pages):
            page_copy(seq, ch, p, sl).start()

    @pl.when(step == 0)
    def _():
        start_all(0, 0, 0)

    nxt = step + 1

    @pl.when(nxt < n_seq * n_chunk)
    def _():
        start_all(nxt // n_chunk, nxt % n_chunk, 1 - slot)

    for p in range(pages):
        page_copy(s_id, c_id, p, slot).wait()

    @pl.when(c_id == 0)
    def _():
        acc_ref[...] = jnp.zeros_like(acc_ref)
        carry_ref[...] = jnp.zeros_like(carry_ref)

    head_row = lax.broadcasted_iota(jnp.int32, (SB_HEADS, SB_W), 0)
    lane = lax.broadcasted_iota(jnp.int32, (SB_HEADS, SB_W), 1)
    own = (lane // HEAD_DIM) == head_row
    q = jnp.broadcast_to(q_ref[0].astype(F32), (SB_HEADS, SB_W))
    q8 = jnp.where(own, q, 0.0).astype(BF16)

    z = jnp.concatenate([_dot(q8, buf[slot, p, 0:SB_W, :].astype(BF16)) for p in range(pages)], axis=1) * LOG2E
    sp = _softplus2(z)
    lm = -sp
    later, total = _suffix_sums(lm)
    a = jnp.exp2(z - sp + later + carry_ref[...]).astype(BF16)
    page = buf.shape[3]
    acc = acc_ref[...]
    for p in range(pages):
        acc = acc + _nt(a[:, p * page:(p + 1) * page], buf[slot, p, SB_W:2 * SB_W, :].astype(BF16))
    acc_ref[...] = acc
    carry_ref[...] += total

    @pl.when(c_id == n_chunk - 1)
    def _():
        o_ref[0] = jnp.sum(jnp.where(own, acc_ref[...], 0.0), axis=0, keepdims=True).astype(BF16)


def _sb_decode(page_table, q, cache_feat):
    db = q.shape[0]
    n_pages = page_table.shape[1]
    page = cache_feat.shape[2]
    pages = min(SB_DEC_PAGES, n_pages)
    assert n_pages % pages == 0 and page % LANES == 0
    seq3 = lambda i, c, pt: (i, 0, 0)
    grid_spec = pltpu.PrefetchScalarGridSpec(
        num_scalar_prefetch=1,
        grid=(db, n_pages // pages),
        in_specs=[pl.BlockSpec((1, 1, SB_W), seq3), pl.BlockSpec(memory_space=pl.ANY)],
        out_specs=pl.BlockSpec((1, 1, SB_W), seq3),
        scratch_shapes=[pltpu.VMEM((2, pages, 2 * SB_W, page), F32), pltpu.SemaphoreType.DMA((2,)),
                        pltpu.VMEM((SB_HEADS, SB_W), F32), pltpu.VMEM((SB_HEADS, 1), F32)],
    )
    return pl.pallas_call(
        functools.partial(_sb_dec_kernel, pages=pages),
        grid_spec=grid_spec,
        out_shape=jax.ShapeDtypeStruct((db, 1, SB_W), BF16),
        compiler_params=_cparams(("arbitrary", "arbitrary")),
        name="sb_decode",
    )(page_table, q, cache_feat)


def _rope_tables(pos):
    half = HEAD_DIM // 2
    inv = jnp.power(ROPE_THETA, -jnp.arange(half, dtype=F32) * (2.0 / HEAD_DIM))
    ang = pos.astype(F32)[:, None] * inv[None, :]
    cos, sin = jnp.cos(ang), jnp.sin(ang)
    reps = LANES // HEAD_DIM
    return jnp.tile(cos, (1, 2 * reps)), jnp.tile(jnp.concatenate([-sin, sin], axis=1), (1, reps))


def _cmp_to_sel(n_cmp, n_sel, rows, cols):
    s = np.arange(n_cmp)[:, None] * D_CMP
    b = np.arange(n_sel)[None, :] * L_SEL
    m = np.zeros((rows, cols), np.float32)
    m[:n_cmp, :n_sel] = (s < b + L_SEL) & (s + L_CMP > b)
    return jnp.asarray(m)


def _feature_major(a):
    n, tok = a.shape[:2]
    return jnp.transpose(a, (0, 2, 3, 4, 1)).reshape(n, -1, tok)


def _token_major(a, feat_shape):
    n, _, tok = a.shape
    return jnp.transpose(a.reshape((n,) + tuple(feat_shape) + (tok,)), (0, 4, 1, 2, 3))


def _round_up(a, b):
    return -(-a // b) * b


def _row_tile(n):
    for tm in (512, 256, 128, 64, 32, 16, 8):
        if n % tm == 0:
            return tm
    raise ValueError(f"row count {n} is not a multiple of 8")


def kernel(x_prompt, x_sample, p_prompt, p_sample, cache_nsa, cache_sb, state_win, page_table, w_in, w_cmp, pe_cmp,
           w_oa, w_ob, w_out, g_pre_mix, g_post_mix, g_pre_ffn, g_post_ffn, w_up, w_down, w_ple, w_ple_gate):
    B, T, D = x_prompt.shape
    DB, TS = x_sample.shape[:2]
    assert TS == 1, "the decode kernels take one new token per sequence"
    depth = w_in.shape[0]
    n_phys, page = cache_nsa.shape[1:3]
    n_pages = page_table.shape[1]
    past = n_pages * page
    w_buf = state_win.shape[2]
    assert T % TOK_PER_ROW == 0 and page % TOK_PER_ROW == 0 and w_buf >= 2
    n2_p = T // TOK_PER_ROW
    n2_s = past // TOK_PER_ROW
    feat = 4 * NSA_KVW

    cos_p, sin_p = _rope_tables(jnp.arange(T, dtype=jnp.int32))
    cos_s, sin_s = _rope_tables(jnp.full((DB,), past, jnp.int32))
    wsel_p = _round_up(T // L_SEL, LANES)
    c2s_p = _cmp_to_sel(n2_p - 1, T // L_SEL, n2_p, wsel_p)
    n_sel_s = past // L_SEL + 1
    wsel_s = _round_up(n_sel_s, LANES)
    c2s_s = _cmp_to_sel(n2_s - 1, n_sel_s, n2_s, wsel_s)
    et = jnp.asarray(np.where(np.arange(T)[:, None] // L_SEL == np.arange(wsel_p)[None, :], NEG_INF, 0.0), BF16)
    tri = jnp.asarray(np.arange(SB_T)[:, None] > np.arange(SB_T)[None, :], BF16)

    tm_p = _row_tile(B * T)
    tm_s = _row_tile(DB)
    xp = x_prompt.reshape(B * T, D)
    xs = x_sample.reshape(DB, D)
    o = np.cumsum((0, NSA_QW, 6 * NSA_KVW, GATE_W, 3 * SB_W, 2 * D))
    nsa_p, sb_p, win_p, nsa_s, sb_s, win_s = [], [], [], [], [], []
    for i in range(depth):
        w = w_in[i]
        w_proj = jnp.concatenate([w[:, o[0]:o[2]], w[:, o[3]:o[4]],
                                  jnp.pad(w[:, o[2]:o[3]], ((0, 0), (0, LANES - GATE_W)))], axis=1).astype(BF16)
        w_mg = w[:, o[4]:o[5]].astype(BF16)
        wc = jnp.repeat(w_cmp[i], NSA_KV, axis=0)
        wblk = jnp.einsum('ab,alde->ladbe', jnp.eye(2 * NSA_KV, dtype=F32), wc).reshape(L_CMP, 2 * NSA_KVW, 2 * NSA_KVW).astype(BF16)
        wlo, whi = wblk[:TOK_PER_ROW], wblk[TOK_PER_ROW:]
        pe = jnp.broadcast_to(pe_cmp[i][:, :, None, :], (L_CMP, 2, NSA_KV, HEAD_DIM)).reshape(L_CMP, 1, 2 * NSA_KVW)
        pelo, pehi = pe[:TOK_PER_ROW], pe[TOK_PER_ROW:]
        g_pre = g_pre_mix[i].reshape(1, D)
        tail_w = (g_pre, w_mg, w_oa[i].astype(BF16), w_ob[i].astype(BF16), w_out[i].astype(BF16), g_post_mix[i].reshape(1, D))
        ffn_w = (g_pre_ffn[i].reshape(1, D), w_up[i].astype(BF16), w_down[i].astype(BF16), g_post_ffn[i].reshape(1, D),
                 w_ple[i].astype(BF16), w_ple_gate[i].astype(BF16))

        (q, nsa, win, gate, sbq, ksel, vsel, kwin, vwin, sbk, sbv, nsa_t, win_t, sbkv_t) = _project(
            xp, g_pre, w_proj, cos_p, sin_p, tm_p)
        b3 = lambda a: a.reshape(B, T, a.shape[-1])
        kc = _compress(nsa.reshape(B, n2_p, TOK_PER_ROW * feat), wlo, whi, pelo, pehi, min(128, n2_p))
        o_a = _nsa_prompt(b3(q), b3(gate), kc, b3(ksel), b3(vsel), b3(kwin), b3(vwin), c2s_p, et)
        o_b = _sb_prompt(b3(sbq), b3(sbk), b3(sbv), tri)
        xp = _mix(xp, o_a.reshape(B * T, NSA_QW), o_b.reshape(B * T, SB_W), *tail_w, tm_p)
        xp = _ffn(xp, p_prompt[i].reshape(B * T, -1), *ffn_w, tm_p)
        nsa_p.append(_token_major(nsa_t, (4, NSA_KV, HEAD_DIM)))
        sb_p.append(_token_major(sbkv_t, (2, SB_HEADS, HEAD_DIM)))
        win_p.append(_token_major(win_t[:, :, T - min(WINDOW, T):], (2, NSA_KV, HEAD_DIM)))

        (q, nsa, win, gate, sbq, _, _, _, _, _, _, nsa_t, _, sbkv_t) = _project(xs, g_pre, w_proj, cos_s, sin_s, tm_s)
        r3 = lambda a: a.reshape(DB, 1, a.shape[-1])
        o_a, win_new = _nsa_decode(page_table, r3(q), r3(gate), r3(nsa), r3(win),
                                   _feature_major(state_win[i]), _feature_major(cache_nsa[i]),
                                   wlo, whi, pelo, pehi, c2s_s, past)
        o_b = _sb_decode(page_table, r3(sbq), _feature_major(cache_sb[i]))
        xs = _mix(xs, o_a.reshape(DB, NSA_QW), o_b.reshape(DB, SB_W), *tail_w, tm_s)
        xs = _ffn(xs, p_sample[i].reshape(DB, -1), *ffn_w, tm_s)
        nsa_s.append(jnp.swapaxes(_token_major(nsa_t, (4, NSA_KV, HEAD_DIM)), 0, 1))
        sb_s.append(jnp.swapaxes(_token_major(sbkv_t, (2, SB_HEADS, HEAD_DIM)), 0, 1))
        win_s.append(jnp.transpose(win_new.reshape(DB, 2, NSA_KV, HEAD_DIM, w_buf), (0, 4, 1, 2, 3)))
    return (xp.reshape(B, T, D), xs.reshape(DB, TS, D), jnp.stack(nsa_p), jnp.stack(sb_p), jnp.stack(win_p),
            jnp.stack(nsa_s), jnp.stack(sb_s), jnp.stack(win_s))
```
